```python
import math
import jax, jax.numpy as jnp
from jax import lax
import numpy as np

D_MODEL = 1024
BATCH = 4
SEQ = 8192
DEPTH = 2

CHUNK = 64
HEAD_DIM = 64
N_HEADS_FOX = 8
N_HEADS_CHUNK = 8
WIDTH_FOX = N_HEADS_FOX * HEAD_DIM
WIDTH_CHUNK = N_HEADS_CHUNK * HEAD_DIM
LEFT_CHUNKS = 8
BAND_CHUNKS = LEFT_CHUNKS + 1
REL_CLIP = 128
Q_BLOCK = 128
D_FF = 2816
CONV_WIDTH = 3
LN_EPS = 1e-5
N_MOD = 6
PROJ_SIZES = (WIDTH_FOX, WIDTH_FOX, WIDTH_FOX, N_HEADS_FOX,
              WIDTH_CHUNK, WIDTH_CHUNK, WIDTH_CHUNK, D_MODEL, D_MODEL)
PROJ_COLS = sum(PROJ_SIZES)

kernel_name = "fox_chunkattn_gated_hybrid_deepnorm_adaln"


def _layer_norm(x, gain=None, bias=None):
    xf = x.astype(jnp.float32)
    mu = jnp.mean(xf, axis=-1, keepdims=True)
    var = jnp.mean(jnp.square(xf - mu), axis=-1, keepdims=True)
    y = (xf - mu) * lax.rsqrt(var + LN_EPS)
    if gain is not None:
        y = y * gain.astype(jnp.float32) + bias.astype(jnp.float32)
    return y.astype(x.dtype)


def _forgetting_attention(q, k, v, f_logit):
    b, s, h, dh = q.shape
    n_blk = s // Q_BLOCK
    log_f = jax.nn.log_sigmoid(f_logit.astype(jnp.float32))
    cum = jnp.cumsum(log_f, axis=1).transpose(0, 2, 1)
    scale = 1.0 / math.sqrt(dh)
    k_pos = jnp.arange(s)
    q_blocks = q.reshape(b, n_blk, Q_BLOCK, h, dh).transpose(1, 0, 2, 3, 4)
    cum_blocks = cum.reshape(b, h, n_blk, Q_BLOCK).transpose(2, 0, 1, 3)
    neg = jnp.finfo(jnp.float32).min

    def block(args):
        qb, cq, i = args
        q_pos = i * Q_BLOCK + jnp.arange(Q_BLOCK)
        logits = jnp.einsum('bqhd,bkhd->bhqk', qb, k).astype(jnp.float32) * scale
        logits = logits + cq[:, :, :, None] - cum[:, :, None, :]
        mask = k_pos[None, :] <= q_pos[:, None]
        logits = jnp.where(mask[None, None], logits, neg)
        p = jax.nn.softmax(logits, axis=-1).astype(v.dtype)
        return jnp.einsum('bhqk,bkhd->bqhd', p, v)

    out = lax.map(block, (q_blocks, cum_blocks, jnp.arange(n_blk)))
    return out.transpose(1, 0, 2, 3, 4).reshape(b, s, h * dh)


def _chunk_band_attention(q, k, v, rel_table):
    b, s, h, dh = q.shape
    n_c = s // CHUNK
    band = BAND_CHUNKS * CHUNK
    qc = q.reshape(b, n_c, CHUNK, h, dh)
    pad = ((0, 0), (LEFT_CHUNKS, 0), (0, 0), (0, 0), (0, 0))
    kp = jnp.pad(k.reshape(b, n_c, CHUNK, h, dh), pad)
    vp = jnp.pad(v.reshape(b, n_c, CHUNK, h, dh), pad)
    band_idx = jnp.arange(n_c)[:, None] + jnp.arange(BAND_CHUNKS)[None, :]
    kb = kp[:, band_idx].reshape(b, n_c, band, h, dh)
    vb = vp[:, band_idx].reshape(b, n_c, band, h, dh)
    valid = jnp.repeat(band_idx >= LEFT_CHUNKS, CHUNK, axis=1)
    q_off = LEFT_CHUNKS * CHUNK + np.arange(CHUNK)
    rel = np.clip(q_off[:, None] - np.arange(band)[None, :], -REL_CLIP, REL_CLIP) + REL_CLIP
    bias = rel_table[:, rel].astype(jnp.float32)
    scale = 1.0 / math.sqrt(dh)
    logits = jnp.einsum('bcqhd,bckhd->bchqk', qc, kb).astype(jnp.float32) * scale
    logits = logits + bias[None, None]
    logits = jnp.where(valid[None, :, None, None, :], logits, jnp.finfo(jnp.float32).min)
    p = jax.nn.softmax(logits, axis=-1).astype(v.dtype)
    out = jnp.einsum('bchqk,bckhd->bcqhd', p, vb)
    return out.reshape(b, s, h * dh)


def _causal_depthwise_conv(u, w, bias):
    s = u.shape[1]
    up = jnp.pad(u, ((0, 0), (CONV_WIDTH - 1, 0), (0, 0)))
    y = bias
    for j in range(CONV_WIDTH):
        y = y + w[j] * up[:, j:j + s]
    return y


def setup_inputs(seed: int = 0) -> dict:
    key = jax.random.key(seed)
    ks = jax.random.split(key, 20)
    beta = (8.0 * DEPTH) ** -0.25
    f32 = jnp.float32
    nrm = lambda k, shape: jax.random.normal(k, shape, f32)
    col_scale = np.concatenate([
        np.ones(2 * WIDTH_FOX), beta * np.ones(WIDTH_FOX), 0.5 * np.ones(N_HEADS_FOX),
        np.ones(2 * WIDTH_CHUNK), beta * np.ones(WIDTH_CHUNK), np.ones(2 * D_MODEL)]).astype(np.float32)
    w_in = nrm(ks[2], (DEPTH, D_MODEL, PROJ_COLS)) * (D_MODEL ** -0.5) * jnp.asarray(col_scale)
    b_f = jnp.linspace(1.0, 5.0, N_HEADS_FOX, dtype=f32)[None, :] + 0.1 * nrm(ks[3], (DEPTH, N_HEADS_FOX))
    rel_bias = 0.5 * nrm(ks[4], (DEPTH, N_HEADS_CHUNK, 2 * REL_CLIP + 1))
    w_br_fox = nrm(ks[5], (DEPTH, WIDTH_FOX, D_MODEL)) * WIDTH_FOX ** -0.5
    w_br_chunk = nrm(ks[6], (DEPTH, WIDTH_CHUNK, D_MODEL)) * WIDTH_CHUNK ** -0.5
    w_out = nrm(ks[7], (DEPTH, D_MODEL, D_MODEL)) * (D_MODEL ** -0.5) * beta
    w_up = nrm(ks[8], (DEPTH, D_MODEL, 2 * D_FF)) * D_MODEL ** -0.5
    conv_w = 0.3 * nrm(ks[9], (DEPTH, CONV_WIDTH, 2 * D_FF)) + jnp.array([0.0, 0.0, 1.0], f32)[None, :, None]
    conv_b = 0.01 * nrm(ks[10], (DEPTH, 2 * D_FF))
    w_down = nrm(ks[11], (DEPTH, D_FF, D_MODEL)) * (D_FF ** -0.5) * beta
    w_ada = 0.2 * nrm(ks[12], (DEPTH, D_MODEL, N_MOD * D_MODEL)) * D_MODEL ** -0.5
    b_ada = 0.01 * nrm(ks[13], (DEPTH, N_MOD * D_MODEL))
    ln1_g = 1.0 + 0.05 * nrm(ks[14], (DEPTH, D_MODEL))
    ln1_b = 0.01 * nrm(ks[15], (DEPTH, D_MODEL))
    ln2_g = 1.0 + 0.05 * nrm(ks[16], (DEPTH, D_MODEL))
    ln2_b = 0.01 * nrm(ks[17], (DEPTH, D_MODEL))
    x = nrm(ks[0], (BATCH, SEQ, D_MODEL))
    c = nrm(ks[1], (BATCH, D_MODEL))
    return {"x": x, "c": c, "w_in": w_in, "b_f": b_f, "rel_bias": rel_bias,
            "w_br_fox": w_br_fox, "w_br_chunk": w_br_chunk, "w_out": w_out,
            "w_up": w_up, "conv_w": conv_w, "conv_b": conv_b, "w_down": w_down,
            "w_ada": w_ada, "b_ada": b_ada, "ln1_g": ln1_g, "ln1_b": ln1_b,
            "ln2_g": ln2_g, "ln2_b": ln2_b}


def reference(x, c, w_in, b_f, rel_bias, w_br_fox, w_br_chunk, w_out, w_up, conv_w,
              conv_b, w_down, w_ada, b_ada, ln1_g, ln1_b, ln2_g, ln2_b):
    alpha = (2.0 * DEPTH) ** 0.25
    b, s, _ = x.shape
    split_points = list(np.cumsum(PROJ_SIZES)[:-1])
    cond = jax.nn.silu(c)
    for l in range(DEPTH):
        mod = cond @ w_ada[l] + b_ada[l]
        sh1, sc1, g1, sh2, sc2, g2 = jnp.split(mod[:, None, :], N_MOD, axis=-1)

        h = _layer_norm(x) * (1.0 + sc1) + sh1
        proj = h @ w_in[l]
        q_a, k_a, v_a, f_a, q_c, k_c, v_c, gate_a, gate_c = jnp.split(proj, split_points, axis=-1)
        heads_a = lambda t: t.reshape(b, s, N_HEADS_FOX, HEAD_DIM)
        heads_c = lambda t: t.reshape(b, s, N_HEADS_CHUNK, HEAD_DIM)
        o_a = _forgetting_attention(heads_a(q_a), heads_a(k_a), heads_a(v_a), f_a + b_f[l])
        o_c = _chunk_band_attention(heads_c(q_c), heads_c(k_c), heads_c(v_c), rel_bias[l])
        merged = (jax.nn.sigmoid(gate_a) * (o_a @ w_br_fox[l])
                  + jax.nn.sigmoid(gate_c) * (o_c @ w_br_chunk[l]))
        mix = merged @ w_out[l]
        x = _layer_norm(alpha * x + (1.0 + g1) * mix, ln1_g[l], ln1_b[l])

        h = _layer_norm(x) * (1.0 + sc2) + sh2
        u = _causal_depthwise_conv(h @ w_up[l], conv_w[l], conv_b[l])
        a, val = jnp.split(u, 2, axis=-1)
        y = (jax.nn.silu(a) * val) @ w_down[l]
        x = _layer_norm(alpha * x + (1.0 + g2) * y, ln2_g[l], ln2_b[l])
    return x
```

```python
import functools
import math

import numpy as np
import jax
import jax.numpy as jnp
from jax import lax
from jax.experimental import pallas as pl
from jax.experimental.pallas import tpu as pltpu

F32 = jnp.float32
BF16 = jnp.bfloat16

D_MODEL = 1024
HEAD_DIM = 64
N_HEADS = 8
WIDTH = N_HEADS * HEAD_DIM
CHUNK = 64
LEFT_CHUNKS = 8
REL_CLIP = 128
D_FF = 2816
CONV_WIDTH = 3
LN_EPS = 1e-5
N_MOD = 6
LANES = 128
PAIR = 2 * HEAD_DIM
N_PAIRS = N_HEADS // 2
MAIN_COLS = 3 * WIDTH + 3 * WIDTH + 2 * D_MODEL
NEG = -1e30
VMEM_LIMIT = 56 * 1024 * 1024

TM_PROJ = 1024
TN_PROJ = 1024
T_CUM = 512
TQ_FOX = 512
TK_FOX = 512
TQ_CHUNK = 256
TM_MIX = 512
TM_FFN = 512
CK_FFN = 256
HALO = 16


def _ln(x):
    mu = jnp.mean(x, axis=-1, keepdims=True)
    xc = x - mu
    var = jnp.mean(xc * xc, axis=-1, keepdims=True)
    return xc * lax.rsqrt(var + LN_EPS)


def _dot(a, b):
    return jnp.dot(a, b, preferred_element_type=F32)


def _dot_nt(a, b):
    return lax.dot_general(a, b, (((1,), (1,)), ((), ())), preferred_element_type=F32)


def _split3(v):
    hi = v.astype(BF16)
    r = v - hi.astype(F32)
    mid = r.astype(BF16)
    lo = (r - mid.astype(F32)).astype(BF16)
    return hi, mid, lo


def _params(*sem):
    return pltpu.CompilerParams(dimension_semantics=sem, vmem_limit_bytes=VMEM_LIMIT)


def _mod_kernel(c_ref, w_ref, b_ref, o_ref):
    c = c_ref[...]
    cond = c * jax.nn.sigmoid(c)
    a_hi, a_mid, _ = _split3(cond)
    w = w_ref[0]
    w_hi = w.astype(BF16)
    w_lo = (w - w_hi.astype(F32)).astype(BF16)
    acc = _dot(a_hi, w_hi) + _dot(a_mid, w_hi) + _dot(a_hi, w_lo)
    o_ref[0] = acc + b_ref[0]


def _ada_mod(c, w_ada, b_ada):
    depth, d, n = w_ada.shape
    b = c.shape[0]
    rows = 8
    c_pad = jnp.pad(c, ((0, rows - b), (0, 0)))
    tn = 1024
    out = pl.pallas_call(
        _mod_kernel,
        grid=(depth, n // tn),
        in_specs=[
            pl.BlockSpec((rows, d), lambda l, j: (0, 0)),
            pl.BlockSpec((1, d, tn), lambda l, j: (l, 0, j)),
            pl.BlockSpec((1, 1, tn), lambda l, j: (l, 0, j)),
        ],
        out_specs=pl.BlockSpec((1, rows, tn), lambda l, j: (l, 0, j)),
        out_shape=jax.ShapeDtypeStruct((depth, rows, n), F32),
        compiler_params=_params("parallel", "parallel"),
        name="ada_mod",
    )(c_pad, w_ada, b_ada.reshape(depth, 1, n))
    return out[:, :b, :]


def _ln_proj_kernel(x_ref, mod_ref, w_ref, wf_ref, o_ref, f_ref, h_ref):
    @pl.when(pl.program_id(1) == 0)
    def _():
        y = _ln(x_ref[...])
        h = (y * (1.0 + mod_ref[0, 1:2, :]) + mod_ref[0, 0:1, :]).astype(BF16)
        h_ref[...] = h
        f_ref[...] = _dot(h, wf_ref[...])

    o_ref[...] = _dot(h_ref[...], w_ref[...]).astype(BF16)


def _ln_proj(xr, mod, w_main, w_f, seq):
    rows, d = xr.shape
    tm, tn = TM_PROJ, TN_PROJ
    n = w_main.shape[1]
    tiles_per_seq = seq // tm
    return pl.pallas_call(
        _ln_proj_kernel,
        grid=(rows // tm, n // tn),
        in_specs=[
            pl.BlockSpec((tm, d), lambda i, j: (i, 0)),
            pl.BlockSpec((1, N_MOD, d), lambda i, j: (i // tiles_per_seq, 0, 0)),
            pl.BlockSpec((d, tn), lambda i, j: (0, j)),
            pl.BlockSpec((d, LANES), lambda i, j: (0, 0)),
        ],
        out_specs=[
            pl.BlockSpec((tm, tn), lambda i, j: (i, j)),
            pl.BlockSpec((tm, LANES), lambda i, j: (i, 0)),
        ],
        out_shape=[
            jax.ShapeDtypeStruct((rows, n), BF16),
            jax.ShapeDtypeStruct((rows, LANES), F32),
        ],
        scratch_shapes=[pltpu.VMEM((tm, d), BF16)],
        compiler_params=_params("parallel", "arbitrary"),
        name="ln_proj",
    )(xr, mod, w_main, w_f)


def _cum_kernel(f_ref, bf_ref, cq_ref, ckt_ref, carry_ref):
    @pl.when(pl.program_id(1) == 0)
    def _():
        carry_ref[...] = jnp.zeros_like(carry_ref)

    z = f_ref[0] + bf_ref[...]
    lf = jnp.minimum(z, 0.0) - jnp.log(1.0 + jnp.exp(-jnp.abs(z)))
    hi, mid, lo = _split3(lf)
    tc = lf.shape[0]
    row = lax.broadcasted_iota(jnp.int32, (tc, tc), 0)
    col = lax.broadcasted_iota(jnp.int32, (tc, tc), 1)
    tri = jnp.where(col <= row, 1.0, 0.0).astype(BF16)
    cum = _dot(tri, hi) + _dot(tri, mid) + _dot(tri, lo) + carry_ref[...]
    cq_ref[0] = cum
    ckt_ref[0] = cum.T[0:N_HEADS, :]
    carry_ref[...] = cum[tc - 1:tc, :]


def _cum(f3, bf_pad):
    b, s, _ = f3.shape
    tc = T_CUM
    return pl.pallas_call(
        _cum_kernel,
        grid=(b, s // tc),
        in_specs=[
            pl.BlockSpec((1, tc, LANES), lambda bi, t: (bi, t, 0)),
            pl.BlockSpec((1, LANES), lambda bi, t: (0, 0)),
        ],
        out_specs=[
            pl.BlockSpec((1, tc, LANES), lambda bi, t: (bi, t, 0)),
            pl.BlockSpec((1, N_HEADS, tc), lambda bi, t: (bi, 0, t)),
        ],
        out_shape=[
            jax.ShapeDtypeStruct((b, s, LANES), F32),
            jax.ShapeDtypeStruct((b, N_HEADS, s), F32),
        ],
        scratch_shapes=[pltpu.VMEM((1, LANES), F32)],
        compiler_params=_params("parallel", "arbitrary"),
        name="cum_forget",
    )(f3, bf_pad)


def _fox_kernel(q_ref, k_ref, v_ref, cq_ref, ckt_ref, o_ref, *, tq, tk):
    p = pl.program_id(1)
    i = pl.program_id(2)
    lane = lax.broadcasted_iota(jnp.int32, (1, PAIR), 1)
    q = q_ref[0]
    cq_tile = cq_ref[0]
    q0 = i * tq
    n_full = q0 // tk
    row_pos = q0 + lax.broadcasted_iota(jnp.int32, (tq, tk), 0)
    col_off = lax.broadcasted_iota(jnp.int32, (tq, tk), 1)
    scale = 1.0 / math.sqrt(HEAD_DIM)

    outs = []
    for hh in range(2):
        h = 2 * p + hh
        in_head = (lane < HEAD_DIM) if hh == 0 else (lane >= HEAD_DIM)
        qm = (jnp.where(in_head, q, jnp.zeros_like(q)).astype(F32) * scale).astype(BF16)
        cq_h = jnp.sum(jnp.where(lane == h, cq_tile, 0.0), axis=1, keepdims=True)

        def step(j, carry, masked):
            m, l, acc = carry
            ks = pl.multiple_of(j * tk, tk)
            kj = k_ref[0, pl.ds(ks, tk), :]
            vj = v_ref[0, pl.ds(ks, tk), :]
            ck = ckt_ref[0, pl.ds(h, 1), pl.ds(ks, tk)]
            s = _dot_nt(qm, kj) - ck
            if masked:
                s = jnp.where(col_off + ks <= row_pos, s, NEG)
            m_new = jnp.maximum(m, jnp.max(s, axis=1, keepdims=True) + cq_h)
            alpha = jnp.exp(m - m_new)
            pexp = jnp.exp(s - (m_new - cq_h))
            l_new = alpha * l + jnp.sum(pexp, axis=1, keepdims=True)
            acc_new = alpha * acc + _dot(pexp.astype(BF16), vj)
            return m_new, l_new, acc_new

        init = (jnp.full((tq, 1), NEG, F32), jnp.zeros((tq, 1), F32), jnp.zeros((tq, PAIR), F32))
        carry = lax.fori_loop(0, n_full, functools.partial(step, masked=False), init)
        _, l, acc = step(n_full, carry, True)
        outs.append(acc / l)

    o_ref[0] = jnp.where(lane < HEAD_DIM, outs[0], outs[1]).astype(BF16)


def _fox(proj3, cq, ckt):
    b, s, _ = proj3.shape
    tq, tk = TQ_FOX, TK_FOX
    assert tk % tq == 0 and s % tk == 0
    return pl.pallas_call(
        functools.partial(_fox_kernel, tq=tq, tk=tk),
        grid=(b, N_PAIRS, s // tq),
        in_specs=[
            pl.BlockSpec((1, tq, PAIR), lambda bi, p, i: (bi, i, p)),
            pl.BlockSpec((1, s, PAIR), lambda bi, p, i: (bi, 0, N_PAIRS + p)),
            pl.BlockSpec((1, s, PAIR), lambda bi, p, i: (bi, 0, 2 * N_PAIRS + p)),
            pl.BlockSpec((1, tq, LANES), lambda bi, p, i: (bi, i, 0)),
            pl.BlockSpec((1, N_HEADS, s), lambda bi, p, i: (bi, 0, 0)),
        ],
        out_specs=pl.BlockSpec((1, tq, PAIR), lambda bi, p, i: (bi, i, p)),
        out_shape=jax.ShapeDtypeStruct((b, s, WIDTH), BF16),
        compiler_params=_params("parallel", "parallel", "arbitrary"),
        name="fox_attn",
    )(proj3, proj3, proj3, cq, ckt)


def _chunk_kernel(q_ref, k_ref, v_ref, bias_ref, o_ref, *, tq, nsub):
    i = pl.program_id(2)
    lane = lax.broadcasted_iota(jnp.int32, (1, PAIR), 1)
    q = q_ref[0]
    scale = 1.0 / math.sqrt(HEAD_DIM)
    outs = []
    for hh in range(2):
        in_head = (lane < HEAD_DIM) if hh == 0 else (lane >= HEAD_DIM)
        qm = (jnp.where(in_head, q, jnp.zeros_like(q)).astype(F32) * scale).astype(BF16)
        logits, starts = [], []
        for sub in range(nsub):
            blk = i - (nsub - 1) + sub
            start = pl.multiple_of(jnp.maximum(blk, 0) * tq, tq)
            kj = k_ref[0, pl.ds(start, tq), :]
            s = _dot_nt(qm, kj) + bias_ref[hh, :, sub * tq:(sub + 1) * tq]
            logits.append(jnp.where(blk >= 0, s, NEG))
            starts.append(start)
        m = functools.reduce(jnp.maximum, [jnp.max(s, axis=1, keepdims=True) for s in logits])
        l = jnp.zeros((tq, 1), F32)
        acc = jnp.zeros((tq, PAIR), F32)
        for s, start in zip(logits, starts):
            pexp = jnp.exp(s - m)
            l = l + jnp.sum(pexp, axis=1, keepdims=True)
            acc = acc + _dot(pexp.astype(BF16), v_ref[0, pl.ds(start, tq), :])
        outs.append(acc / l)
    o_ref[0] = jnp.where(lane < HEAD_DIM, outs[0], outs[1]).astype(BF16)


def _chunk_bias_table(rel_table, tq):
    left = LEFT_CHUNKS * CHUNK
    i = np.arange(tq)[:, None]
    j = np.arange(tq + left)[None, :]
    rel = np.clip(left + i - j, -REL_CLIP, REL_CLIP) + REL_CLIP
    qc = i // CHUNK
    kc = j // CHUNK
    band = (kc >= qc) & (kc <= qc + LEFT_CHUNKS)
    bias = rel_table[:, rel].astype(F32)
    return jnp.where(jnp.asarray(band)[None], bias, NEG)


def _chunk_attn(proj3, bias_table):
    b, s, _ = proj3.shape
    tq = TQ_CHUNK
    left = LEFT_CHUNKS * CHUNK
    assert left % tq == 0 and tq % CHUNK == 0
    nsub = left // tq + 1
    col0 = 3 * N_PAIRS
    return pl.pallas_call(
        functools.partial(_chunk_kernel, tq=tq, nsub=nsub),
        grid=(b, N_PAIRS, s // tq),
        in_specs=[
            pl.BlockSpec((1, tq, PAIR), lambda bi, p, i: (bi, i, col0 + p)),
            pl.BlockSpec((1, s, PAIR), lambda bi, p, i: (bi, 0, col0 + N_PAIRS + p)),
            pl.BlockSpec((1, s, PAIR), lambda bi, p, i: (bi, 0, col0 + 2 * N_PAIRS + p)),
            pl.BlockSpec((2, tq, tq + left), lambda bi, p, i: (p, 0, 0)),
        ],
        out_specs=pl.BlockSpec((1, tq, PAIR), lambda bi, p, i: (bi, i, p)),
        out_shape=jax.ShapeDtypeStruct((b, s, WIDTH), BF16),
        compiler_params=_params("parallel", "parallel", "arbitrary"),
        name="chunk_attn",
    )(proj3, proj3, proj3, bias_table)


def _mix_kernel(oa_ref, oc_ref, ga_ref, gc_ref, x_ref, mod_ref, wa_ref, wc_ref, wo_ref,
                g_ref, b_ref, o_ref, *, alpha):
    ya = _dot(oa_ref[...], wa_ref[...])
    yc = _dot(oc_ref[...], wc_ref[...])
    merged = (jax.nn.sigmoid(ga_ref[...].astype(F32)) * ya
              + jax.nn.sigmoid(gc_ref[...].astype(F32)) * yc)
    mix = _dot(merged.astype(BF16), wo_ref[...])
    z = alpha * x_ref[...] + (1.0 + mod_ref[0, 2:3, :]) * mix
    o_ref[...] = _ln(z) * g_ref[...] + b_ref[...]


def _mix(oa, oc, proj, xr, mod, w_a, w_c, w_o, gain, bias, seq, alpha):
    rows, d = xr.shape
    tm = TM_MIX
    tiles_per_seq = seq // tm
    gate_blk = (2 * 3 * WIDTH) // d
    const = lambda i: (0, 0)
    return pl.pallas_call(
        functools.partial(_mix_kernel, alpha=alpha),
        grid=(rows // tm,),
        in_specs=[
            pl.BlockSpec((tm, WIDTH), lambda i: (i, 0)),
            pl.BlockSpec((tm, WIDTH), lambda i: (i, 0)),
            pl.BlockSpec((tm, d), lambda i: (i, gate_blk)),
            pl.BlockSpec((tm, d), lambda i: (i, gate_blk + 1)),
            pl.BlockSpec((tm, d), lambda i: (i, 0)),
            pl.BlockSpec((1, N_MOD, d), lambda i: (i // tiles_per_seq, 0, 0)),
            pl.BlockSpec((WIDTH, d), const),
            pl.BlockSpec((WIDTH, d), const),
            pl.BlockSpec((d, d), const),
            pl.BlockSpec((1, d), const),
            pl.BlockSpec((1, d), const),
        ],
        out_specs=pl.BlockSpec((tm, d), lambda i: (i, 0)),
        out_shape=jax.ShapeDtypeStruct((rows, d), F32),
        compiler_params=_params("parallel"),
        name="mix_out",
    )(oa, oc, proj, proj, xr, mod, w_a, w_c, w_o, gain, bias)


def _ffn_kernel(x_ref, xh_ref, mod_ref, wa_ref, wv_ref, cwa_ref, cwv_ref, cba_ref, cbv_ref,
                wd_ref, g_ref, b_ref, o_ref, h_ref, acc_ref, *, alpha, tm, nck):
    t = pl.program_id(1)
    x = x_ref[0]
    sc = 1.0 + mod_ref[0, 4:5, :]
    sh = mod_ref[0, 3:4, :]
    h_ref[HALO:, :] = (_ln(x) * sc + sh).astype(BF16)
    keep = jnp.where(t > 0, 1.0, 0.0)
    h_ref[0:HALO, :] = ((_ln(xh_ref[0]) * sc + sh) * keep).astype(BF16)
    acc_ref[...] = jnp.zeros_like(acc_ref)

    def conv(u, cw, cb):
        lo = HALO - (CONV_WIDTH - 1)
        y = cb
        for j in range(CONV_WIDTH):
            y = y + cw[j:j + 1, :] * u[lo + j:lo + j + tm, :]
        return y

    def body(c, carry):
        he = h_ref[...]
        a = conv(_dot(he, wa_ref[c]), cwa_ref[c], cba_ref[c])
        v = conv(_dot(he, wv_ref[c]), cwv_ref[c], cbv_ref[c])
        act = (a * jax.nn.sigmoid(a) * v).astype(BF16)
        acc_ref[...] += _dot(act, wd_ref[c])
        return carry

    lax.fori_loop(0, nck, body, 0)
    z = alpha * x + (1.0 + mod_ref[0, 5:6, :]) * acc_ref[...]
    o_ref[0] = _ln(z) * g_ref[...] + b_ref[...]


def _ffn(x3, mod, wa, wv, cwa, cwv, cba, cbv, wd, gain, bias, alpha):
    b, s, d = x3.shape
    tm = TM_FFN
    nck, _, ck = wa.shape
    halo_per_tile = tm // HALO
    c3 = lambda bi, t: (0, 0, 0)
    c2 = lambda bi, t: (0, 0)
    return pl.pallas_call(
        functools.partial(_ffn_kernel, alpha=alpha, tm=tm, nck=nck),
        grid=(b, s // tm),
        in_specs=[
            pl.BlockSpec((1, tm, d), lambda bi, t: (bi, t, 0)),
            pl.BlockSpec((1, HALO, d), lambda bi, t: (bi, jnp.maximum(t * halo_per_tile - 1, 0), 0)),
            pl.BlockSpec((1, N_MOD, d), lambda bi, t: (bi, 0, 0)),
            pl.BlockSpec((nck, d, ck), c3),
            pl.BlockSpec((nck, d, ck), c3),
            pl.BlockSpec((nck, CONV_WIDTH, ck), c3),
            pl.BlockSpec((nck, CONV_WIDTH, ck), c3),
            pl.BlockSpec((nck, 1, ck), c3),
            pl.BlockSpec((nck, 1, ck), c3),
            pl.BlockSpec((nck, ck, d), c3),
            pl.BlockSpec((1, d), c2),
            pl.BlockSpec((1, d), c2),
        ],
        out_specs=pl.BlockSpec((1, tm, d), lambda bi, t: (bi, t, 0)),
        out_shape=jax.ShapeDtypeStruct((b, s, d), F32),
        scratch_shapes=[pltpu.VMEM((tm + HALO, d), BF16), pltpu.VMEM((tm, d), F32)],
        compiler_params=_params("parallel", "parallel"),
        name="conv_ffn",
    )(x3, x3, mod, wa, wv, cwa, cwv, cba, cbv, wd, gain, bias)


def _chunked_cols(w, ck):
    k, n = w.shape
    return w.reshape(k, n // ck, ck).transpose(1, 0, 2)


def kernel(x, c, w_in, b_f, rel_bias, w_br_fox, w_br_chunk, w_out, w_up, conv_w, conv_b, w_down,
           w_ada, b_ada, ln1_g, ln1_b, ln2_g, ln2_b):
    b, s, d = x.shape
    depth = w_in.shape[0]
    alpha = (2.0 * depth) ** 0.25
    rows = b * s
    mod_all = _ada_mod(c, w_ada, b_ada).reshape(depth, b, N_MOD, d)

    f0 = 3 * WIDTH
    f1 = f0 + N_HEADS
    nck = D_FF // CK_FFN
    xr = x.reshape(rows, d)
    for l in range(depth):
        mod = mod_all[l]
        w_main = jnp.concatenate([w_in[l, :, :f0], w_in[l, :, f1:]], axis=1).astype(BF16)
        w_f = jnp.pad(w_in[l, :, f0:f1], ((0, 0), (0, LANES - N_HEADS))).astype(BF16)
        bf_pad = jnp.pad(b_f[l], (0, LANES - N_HEADS)).reshape(1, LANES)

        proj, f = _ln_proj(xr, mod, w_main, w_f, s)
        cq, ckt = _cum(f.reshape(b, s, LANES), bf_pad)
        proj3 = proj.reshape(b, s, MAIN_COLS)
        o_a = _fox(proj3, cq, ckt)
        o_c = _chunk_attn(proj3, _chunk_bias_table(rel_bias[l], TQ_CHUNK))
        xr = _mix(o_a.reshape(rows, WIDTH), o_c.reshape(rows, WIDTH), proj, xr, mod,
                  w_br_fox[l].astype(BF16), w_br_chunk[l].astype(BF16), w_out[l].astype(BF16),
                  ln1_g[l].reshape(1, d), ln1_b[l].reshape(1, d), s, alpha)

        wa = _chunked_cols(w_up[l, :, :D_FF].astype(BF16), CK_FFN)
        wv = _chunked_cols(w_up[l, :, D_FF:].astype(BF16), CK_FFN)
        cwa = _chunked_cols(conv_w[l, :, :D_FF], CK_FFN)
        cwv = _chunked_cols(conv_w[l, :, D_FF:], CK_FFN)
        cba = conv_b[l, :D_FF].reshape(nck, 1, CK_FFN)
        cbv = conv_b[l, D_FF:].reshape(nck, 1, CK_FFN)
        wd = w_down[l].astype(BF16).reshape(nck, CK_FFN, d)
        x3 = _ffn(xr.reshape(b, s, d), mod, wa, wv, cwa, cwv, cba, cbv, wd,
                  ln2_g[l].reshape(1, d), ln2_b[l].reshape(1, d), alpha)
        xr = x3.reshape(rows, d)
    return xr.reshape(b, s, d)
```

```python
import functools
import math

import numpy as np
import jax
import jax.numpy as jnp
from jax import lax
from jax.experimental import pallas as pl
from jax.experimental.pallas import tpu as pltpu

F32 = jnp.float32
BF16 = jnp.bfloat16

D_MODEL = 1024
HEAD_DIM = 64
N_HEADS = 8
WIDTH = N_HEADS * HEAD_DIM
CHUNK = 64
LEFT_CHUNKS = 8
REL_CLIP = 128
D_FF = 2816
CONV_WIDTH = 3
LN_EPS = 1e-5
N_MOD = 6
LANES = 128
PAIR = 2 * HEAD_DIM
N_PAIRS = N_HEADS // 2
MAIN_COLS = 3 * WIDTH + 3 * WIDTH + 2 * D_MODEL
LOG2E = math.log2(math.e)
NEG = -1e30
VMEM_LIMIT = 56 * 1024 * 1024

TM_PROJ = 1024
TN_PROJ = 1024
T_CUM = 512
TQ_FOX = 512
TK_FOX = 512
TQ_CHUNK = 256
TM_MIX = 512
TM_FFN = 512
CK_FFN = 256
HALO = 16


def _ln(x):
    mu = jnp.mean(x, axis=-1, keepdims=True)
    xc = x - mu
    var = jnp.mean(xc * xc, axis=-1, keepdims=True)
    return xc * lax.rsqrt(var + LN_EPS)


def _dot(a, b):
    return jnp.dot(a, b, preferred_element_type=F32)


def _dot_nt(a, b):
    return lax.dot_general(a, b, (((1,), (1,)), ((), ())), preferred_element_type=F32)


def _split3(v):
    hi = v.astype(BF16)
    r = v - hi.astype(F32)
    mid = r.astype(BF16)
    lo = (r - mid.astype(F32)).astype(BF16)
    return hi, mid, lo


def _params(*sem):
    return pltpu.CompilerParams(dimension_semantics=sem, vmem_limit_bytes=VMEM_LIMIT)


def _mod_kernel(c_ref, w_ref, b_ref, o_ref):
    c = c_ref[...]
    cond = c * jax.nn.sigmoid(c)
    a_hi, a_mid, _ = _split3(cond)
    w = w_ref[0]
    w_hi = w.astype(BF16)
    w_lo = (w - w_hi.astype(F32)).astype(BF16)
    acc = _dot(a_hi, w_hi) + _dot(a_mid, w_hi) + _dot(a_hi, w_lo)
    o_ref[0] = acc + b_ref[0]


def _ada_mod(c, w_ada, b_ada):
    depth, d, n = w_ada.shape
    b = c.shape[0]
    rows = 8
    c_pad = jnp.pad(c, ((0, rows - b), (0, 0)))
    tn = 1024
    out = pl.pallas_call(
        _mod_kernel,
        grid=(depth, n // tn),
        in_specs=[
            pl.BlockSpec((rows, d), lambda l, j: (0, 0)),
            pl.BlockSpec((1, d, tn), lambda l, j: (l, 0, j)),
            pl.BlockSpec((1, 1, tn), lambda l, j: (l, 0, j)),
        ],
        out_specs=pl.BlockSpec((1, rows, tn), lambda l, j: (l, 0, j)),
        out_shape=jax.ShapeDtypeStruct((depth, rows, n), F32),
        compiler_params=_params("parallel", "parallel"),
        name="ada_mod",
    )(c_pad, w_ada, b_ada.reshape(depth, 1, n))
    return out[:, :b, :]


def _ln_proj_kernel(x_ref, mod_ref, w_ref, wf_ref, o_ref, f_ref, h_ref):
    @pl.when(pl.program_id(1) == 0)
    def _():
        y = _ln(x_ref[...])
        h = (y * (1.0 + mod_ref[0, 1:2, :]) + mod_ref[0, 0:1, :]).astype(BF16)
        h_ref[...] = h
        f_ref[...] = _dot(h, wf_ref[...])

    o_ref[...] = _dot(h_ref[...], w_ref[...]).astype(BF16)


def _ln_proj(xr, mod, w_main, w_f, seq):
    rows, d = xr.shape
    tm, tn = TM_PROJ, TN_PROJ
    n = w_main.shape[1]
    tiles_per_seq = seq // tm
    return pl.pallas_call(
        _ln_proj_kernel,
        grid=(rows // tm, n // tn),
        in_specs=[
            pl.BlockSpec((tm, d), lambda i, j: (i, 0)),
            pl.BlockSpec((1, N_MOD, d), lambda i, j: (i // tiles_per_seq, 0, 0)),
            pl.BlockSpec((d, tn), lambda i, j: (0, j)),
            pl.BlockSpec((d, LANES), lambda i, j: (0, 0)),
        ],
        out_specs=[
            pl.BlockSpec((tm, tn), lambda i, j: (i, j)),
            pl.BlockSpec((tm, LANES), lambda i, j: (i, 0)),
        ],
        out_shape=[
            jax.ShapeDtypeStruct((rows, n), BF16),
            jax.ShapeDtypeStruct((rows, LANES), F32),
        ],
        scratch_shapes=[pltpu.VMEM((tm, d), BF16)],
        compiler_params=_params("parallel", "arbitrary"),
        name="ln_proj",
    )(xr, mod, w_main, w_f)


def _fox_prep_kernel(f_ref, bf_ref, v_ref, cqt_ref, fs_ref, vt_ref, carry_ref):
    @pl.when(pl.program_id(1) == 0)
    def _():
        carry_ref[...] = jnp.zeros_like(carry_ref)

    z = f_ref[0] + bf_ref[...]
    lf = jnp.minimum(z, 0.0) - jnp.log(1.0 + jnp.exp(-jnp.abs(z)))
    hi, mid, lo = _split3(lf)
    tc = lf.shape[0]
    row = lax.broadcasted_iota(jnp.int32, (tc, tc), 0)
    col = lax.broadcasted_iota(jnp.int32, (tc, tc), 1)
    tri = jnp.where(col <= row, 1.0, 0.0).astype(BF16)
    cum = _dot(tri, hi) + _dot(tri, mid) + _dot(tri, lo) + carry_ref[...]
    carry_ref[...] = cum[tc - 1:tc, :]

    cum2 = cum * LOG2E
    cqt_ref[0] = cum2.T[0:N_HEADS, :]
    neg = -cum2
    n_hi = neg.astype(BF16).astype(F32)
    r1 = neg - n_hi
    n_mid = r1.astype(BF16).astype(F32)
    n_lo = r1 - n_mid
    lane = lax.broadcasted_iota(jnp.int32, (1, LANES), 1)
    placed = jnp.where(lane < N_HEADS, n_hi,
                       jnp.where(lane < 2 * N_HEADS, pltpu.roll(n_mid, N_HEADS, axis=1),
                                 jnp.where(lane < 3 * N_HEADS, pltpu.roll(n_lo, 2 * N_HEADS, axis=1), 0.0)))
    fs_ref[0] = placed.astype(BF16)

    for p in range(N_PAIRS):
        vt = v_ref[0, :, p * PAIR:(p + 1) * PAIR].astype(F32).T
        vt_ref[0, 2 * p] = vt[0:HEAD_DIM].astype(BF16)
        vt_ref[0, 2 * p + 1] = vt[HEAD_DIM:].astype(BF16)


def _fox_prep(f3, bf_pad, proj3):
    b, s, _ = f3.shape
    tc = T_CUM
    v_blk = (2 * WIDTH) // WIDTH
    return pl.pallas_call(
        _fox_prep_kernel,
        grid=(b, s // tc),
        in_specs=[
            pl.BlockSpec((1, tc, LANES), lambda bi, t: (bi, t, 0)),
            pl.BlockSpec((1, LANES), lambda bi, t: (0, 0)),
            pl.BlockSpec((1, tc, WIDTH), lambda bi, t: (bi, t, v_blk)),
        ],
        out_specs=[
            pl.BlockSpec((1, N_HEADS, tc), lambda bi, t: (bi, 0, t)),
            pl.BlockSpec((1, tc, LANES), lambda bi, t: (bi, t, 0)),
            pl.BlockSpec((1, N_HEADS, HEAD_DIM, tc), lambda bi, t: (bi, 0, 0, t)),
        ],
        out_shape=[
            jax.ShapeDtypeStruct((b, N_HEADS, s), F32),
            jax.ShapeDtypeStruct((b, s, LANES), BF16),
            jax.ShapeDtypeStruct((b, N_HEADS, HEAD_DIM, s), BF16),
        ],
        scratch_shapes=[pltpu.VMEM((1, LANES), F32)],
        compiler_params=_params("parallel", "arbitrary"),
        name="fox_prep",
    )(f3, bf_pad, proj3)


def _fox_kernel(q_ref, k_ref, fs_ref, vt_ref, cqt_ref, o_ref, *, tq, tk):
    p = pl.program_id(1)
    i = pl.program_id(2)
    q0 = i * tq
    n_full = q0 // tk
    qt = q_ref[0].astype(F32).T
    row = lax.broadcasted_iota(jnp.int32, (PAIR, 1), 0)
    qts, cqs = [], []
    for hh in range(2):
        h = 2 * p + hh
        in_head = (row < HEAD_DIM) if hh == 0 else (row >= HEAD_DIM)
        q_rows = jnp.where(in_head, qt, 0.0)
        pick = (row == h) | (row == h + N_HEADS) | (row == h + 2 * N_HEADS)
        one_rows = jnp.broadcast_to(jnp.where(pick, 1.0, 0.0), (PAIR, tq))
        qts.append(jnp.concatenate([q_rows, one_rows], axis=0).astype(BF16))
        cqs.append(cqt_ref[0, pl.ds(h, 1), :])
    qt_both = jnp.concatenate(qts, axis=1)
    cq = jnp.concatenate(cqs, axis=1)
    key_off = lax.broadcasted_iota(jnp.int32, (tk, 1), 0)
    q_pos = q0 + lax.broadcasted_iota(jnp.int32, (1, 2 * tq), 1) % tq

    def step(j, carry, masked):
        m, l, acc0, acc1 = carry
        ks = pl.multiple_of(j * tk, tk)
        kf = jnp.concatenate([k_ref[0, pl.ds(ks, tk), :], fs_ref[0, pl.ds(ks, tk), :]], axis=1)
        st = _dot(kf, qt_both)
        if masked:
            st = jnp.where(key_off + ks <= q_pos, st, NEG)
        m_new = jnp.maximum(m, jnp.max(st, axis=0, keepdims=True) + cq)
        alpha = jnp.exp2(m - m_new)
        pexp = jnp.exp2(st - (m_new - cq))
        l_new = alpha * l + jnp.sum(pexp, axis=0, keepdims=True)
        pb = pexp.astype(BF16)
        acc0 = alpha[:, :tq] * acc0 + _dot(vt_ref[0, 0, :, pl.ds(ks, tk)], pb[:, :tq])
        acc1 = alpha[:, tq:] * acc1 + _dot(vt_ref[0, 1, :, pl.ds(ks, tk)], pb[:, tq:])
        return m_new, l_new, acc0, acc1

    init = (jnp.full((1, 2 * tq), NEG, F32), jnp.zeros((1, 2 * tq), F32),
            jnp.zeros((HEAD_DIM, tq), F32), jnp.zeros((HEAD_DIM, tq), F32))
    carry = lax.fori_loop(0, n_full, functools.partial(step, masked=False), init)
    _, l, acc0, acc1 = step(n_full, carry, True)
    inv = 1.0 / l
    ot = jnp.concatenate([acc0 * inv[:, :tq], acc1 * inv[:, tq:]], axis=0)
    o_ref[0] = ot.T.astype(BF16)


def _fox(proj3, fs, vt, cqt):
    b, s, _ = proj3.shape
    tq, tk = TQ_FOX, TK_FOX
    assert tk % tq == 0 and s % tk == 0
    return pl.pallas_call(
        functools.partial(_fox_kernel, tq=tq, tk=tk),
        grid=(b, N_PAIRS, s // tq),
        in_specs=[
            pl.BlockSpec((1, tq, PAIR), lambda bi, p, i: (bi, i, p)),
            pl.BlockSpec((1, s, PAIR), lambda bi, p, i: (bi, 0, N_PAIRS + p)),
            pl.BlockSpec((1, s, LANES), lambda bi, p, i: (bi, 0, 0)),
            pl.BlockSpec((1, 2, HEAD_DIM, s), lambda bi, p, i: (bi, p, 0, 0)),
            pl.BlockSpec((1, N_HEADS, tq), lambda bi, p, i: (bi, 0, i)),
        ],
        out_specs=pl.BlockSpec((1, tq, PAIR), lambda bi, p, i: (bi, i, p)),
        out_shape=jax.ShapeDtypeStruct((b, s, WIDTH), BF16),
        compiler_params=_params("parallel", "parallel", "arbitrary"),
        name="fox_attn",
    )(proj3, proj3, fs, vt, cqt)


def _chunk_kernel(q_ref, k_ref, v_ref, bias_ref, o_ref, *, tq, nsub):
    i = pl.program_id(2)
    lane = lax.broadcasted_iota(jnp.int32, (1, PAIR), 1)
    q = q_ref[0]
    scale = 1.0 / math.sqrt(HEAD_DIM)
    outs = []
    for hh in range(2):
        in_head = (lane < HEAD_DIM) if hh == 0 else (lane >= HEAD_DIM)
        qm = (jnp.where(in_head, q, jnp.zeros_like(q)).astype(F32) * scale).astype(BF16)
        logits, starts = [], []
        for sub in range(nsub):
            blk = i - (nsub - 1) + sub
            start = pl.multiple_of(jnp.maximum(blk, 0) * tq, tq)
            kj = k_ref[0, pl.ds(start, tq), :]
            s = _dot_nt(qm, kj) + bias_ref[hh, :, sub * tq:(sub + 1) * tq]
            logits.append(jnp.where(blk >= 0, s, NEG))
            starts.append(start)
        m = functools.reduce(jnp.maximum, [jnp.max(s, axis=1, keepdims=True) for s in logits])
        l = jnp.zeros((tq, 1), F32)
        acc = jnp.zeros((tq, PAIR), F32)
        for s, start in zip(logits, starts):
            pexp = jnp.exp(s - m)
            l = l + jnp.sum(pexp, axis=1, keepdims=True)
            acc = acc + _dot(pexp.astype(BF16), v_ref[0, pl.ds(start, tq), :])
        outs.append(acc / l)
    o_ref[0] = jnp.where(lane < HEAD_DIM, outs[0], outs[1]).astype(BF16)


def _chunk_bias_table(rel_table, tq):
    left = LEFT_CHUNKS * CHUNK
    w = tq + left
    period = tq + w - 1
    heads = rel_table.shape[0]
    top = rel_table[:, 2 * REL_CLIP:]
    bot = rel_table[:, :1]
    n_top = left - REL_CLIP + 1
    n_bot = w - n_top - 2 * REL_CLIP
    u = jnp.concatenate([
        jnp.broadcast_to(top, (heads, n_top)),
        rel_table[:, :2 * REL_CLIP][:, ::-1],
        jnp.broadcast_to(bot, (heads, n_bot)),
        jnp.broadcast_to(top, (heads, tq - 1)),
    ], axis=1).astype(F32)
    flat = jnp.tile(u, (1, tq + 1))[:, :tq * (period - 1)]
    bias = flat.reshape(heads, tq, period - 1)[:, :, :w]
    qc = np.arange(tq)[:, None] // CHUNK
    kc = np.arange(w)[None, :] // CHUNK
    band = (kc >= qc) & (kc <= qc + LEFT_CHUNKS)
    return jnp.where(jnp.asarray(band)[None], bias, NEG)


def _chunk_attn(proj3, bias_table):
    b, s, _ = proj3.shape
    tq = TQ_CHUNK
    left = LEFT_CHUNKS * CHUNK
    assert left % tq == 0 and tq % CHUNK == 0
    nsub = left // tq + 1
    col0 = 3 * N_PAIRS
    return pl.pallas_call(
        functools.partial(_chunk_kernel, tq=tq, nsub=nsub),
        grid=(b, N_PAIRS, s // tq),
        in_specs=[
            pl.BlockSpec((1, tq, PAIR), lambda bi, p, i: (bi, i, col0 + p)),
            pl.BlockSpec((1, s, PAIR), lambda bi, p, i: (bi, 0, col0 + N_PAIRS + p)),
            pl.BlockSpec((1, s, PAIR), lambda bi, p, i: (bi, 0, col0 + 2 * N_PAIRS + p)),
            pl.BlockSpec((2, tq, tq + left), lambda bi, p, i: (p, 0, 0)),
        ],
        out_specs=pl.BlockSpec((1, tq, PAIR), lambda bi, p, i: (bi, i, p)),
        out_shape=jax.ShapeDtypeStruct((b, s, WIDTH), BF16),
        compiler_params=_params("parallel", "parallel", "arbitrary"),
        name="chunk_attn",
    )(proj3, proj3, proj3, bias_table)


def _mix_kernel(oa_ref, oc_ref, ga_ref, gc_ref, x_ref, mod_ref, wa_ref, wc_ref, wo_ref,
                g_ref, b_ref, o_ref, *, alpha):
    ya = _dot(oa_ref[...], wa_ref[...])
    yc = _dot(oc_ref[...], wc_ref[...])
    merged = (jax.nn.sigmoid(ga_ref[...].astype(F32)) * ya
              + jax.nn.sigmoid(gc_ref[...].astype(F32)) * yc)
    mix = _dot(merged.astype(BF16), wo_ref[...])
    z = alpha * x_ref[...] + (1.0 + mod_ref[0, 2:3, :]) * mix
    o_ref[...] = _ln(z) * g_ref[...] + b_ref[...]


def _mix(oa, oc, proj, xr, mod, w_a, w_c, w_o, gain, bias, seq, alpha):
    rows, d = xr.shape
    tm = TM_MIX
    tiles_per_seq = seq // tm
    gate_blk = (2 * 3 * WIDTH) // d
    const = lambda i: (0, 0)
    return pl.pallas_call(
        functools.partial(_mix_kernel, alpha=alpha),
        grid=(rows // tm,),
        in_specs=[
            pl.BlockSpec((tm, WIDTH), lambda i: (i, 0)),
            pl.BlockSpec((tm, WIDTH), lambda i: (i, 0)),
            pl.BlockSpec((tm, d), lambda i: (i, gate_blk)),
            pl.BlockSpec((tm, d), lambda i: (i, gate_blk + 1)),
            pl.BlockSpec((tm, d), lambda i: (i, 0)),
            pl.BlockSpec((1, N_MOD, d), lambda i: (i // tiles_per_seq, 0, 0)),
            pl.BlockSpec((WIDTH, d), const),
            pl.BlockSpec((WIDTH, d), const),
            pl.BlockSpec((d, d), const),
            pl.BlockSpec((1, d), const),
            pl.BlockSpec((1, d), const),
        ],
        out_specs=pl.BlockSpec((tm, d), lambda i: (i, 0)),
        out_shape=jax.ShapeDtypeStruct((rows, d), F32),
        compiler_params=_params("parallel"),
        name="mix_out",
    )(oa, oc, proj, proj, xr, mod, w_a, w_c, w_o, gain, bias)


def _ffn_kernel(x_ref, xh_ref, mod_ref, wa_ref, wv_ref, cwa_ref, cwv_ref, cba_ref, cbv_ref,
                wd_ref, g_ref, b_ref, o_ref, h_ref, acc_ref, *, alpha, tm, nck):
    t = pl.program_id(1)
    x = x_ref[0]
    sc = 1.0 + mod_ref[0, 4:5, :]
    sh = mod_ref[0, 3:4, :]
    h_ref[HALO:, :] = (_ln(x) * sc + sh).astype(BF16)
    keep = jnp.where(t > 0, 1.0, 0.0)
    h_ref[0:HALO, :] = ((_ln(xh_ref[0]) * sc + sh) * keep).astype(BF16)
    acc_ref[...] = jnp.zeros_like(acc_ref)

    def conv(u, cw, cb):
        lo = HALO - (CONV_WIDTH - 1)
        y = cb
        for j in range(CONV_WIDTH):
            y = y + cw[j:j + 1, :] * u[lo + j:lo + j + tm, :]
        return y

    def body(c, carry):
        he = h_ref[...]
        a = conv(_dot(he, wa_ref[c]), cwa_ref[c], cba_ref[c])
        v = conv(_dot(he, wv_ref[c]), cwv_ref[c], cbv_ref[c])
        act = (a * jax.nn.sigmoid(a) * v).astype(BF16)
        acc_ref[...] += _dot(act, wd_ref[c])
        return carry

    lax.fori_loop(0, nck, body, 0, unroll=True)
    z = alpha * x + (1.0 + mod_ref[0, 5:6, :]) * acc_ref[...]
    o_ref[0] = _ln(z) * g_ref[...] + b_ref[...]


def _ffn(x3, mod, wa, wv, cwa, cwv, cba, cbv, wd, gain, bias, alpha):
    b, s, d = x3.shape
    tm = TM_FFN
    nck, _, ck = wa.shape
    halo_per_tile = tm // HALO
    c3 = lambda bi, t: (0, 0, 0)
    c2 = lambda bi, t: (0, 0)
    return pl.pallas_call(
        functools.partial(_ffn_kernel, alpha=alpha, tm=tm, nck=nck),
        grid=(b, s // tm),
        in_specs=[
            pl.BlockSpec((1, tm, d), lambda bi, t: (bi, t, 0)),
            pl.BlockSpec((1, HALO, d), lambda bi, t: (bi, jnp.maximum(t * halo_per_tile - 1, 0), 0)),
            pl.BlockSpec((1, N_MOD, d), lambda bi, t: (bi, 0, 0)),
            pl.BlockSpec((nck, d, ck), c3),
            pl.BlockSpec((nck, d, ck), c3),
            pl.BlockSpec((nck, CONV_WIDTH, ck), c3),
            pl.BlockSpec((nck, CONV_WIDTH, ck), c3),
            pl.BlockSpec((nck, 1, ck), c3),
            pl.BlockSpec((nck, 1, ck), c3),
            pl.BlockSpec((nck, ck, d), c3),
            pl.BlockSpec((1, d), c2),
            pl.BlockSpec((1, d), c2),
        ],
        out_specs=pl.BlockSpec((1, tm, d), lambda bi, t: (bi, t, 0)),
        out_shape=jax.ShapeDtypeStruct((b, s, d), F32),
        scratch_shapes=[pltpu.VMEM((tm + HALO, d), BF16), pltpu.VMEM((tm, d), F32)],
        compiler_params=_params("parallel", "parallel"),
        name="conv_ffn",
    )(x3, x3, mod, wa, wv, cwa, cwv, cba, cbv, wd, gain, bias)


def _chunked_cols(w, ck):
    k, n = w.shape
    return w.reshape(k, n // ck, ck).transpose(1, 0, 2)


def kernel(x, c, w_in, b_f, rel_bias, w_br_fox, w_br_chunk, w_out, w_up, conv_w, conv_b, w_down,
           w_ada, b_ada, ln1_g, ln1_b, ln2_g, ln2_b):
    b, s, d = x.shape
    depth = w_in.shape[0]
    alpha = (2.0 * depth) ** 0.25
    rows = b * s
    mod_all = _ada_mod(c, w_ada, b_ada).reshape(depth, b, N_MOD, d)

    f0 = 3 * WIDTH
    f1 = f0 + N_HEADS
    nck = D_FF // CK_FFN
    xr = x.reshape(rows, d)
    for l in range(depth):
        mod = mod_all[l]
        q_scale = LOG2E / math.sqrt(HEAD_DIM)
        w_main = jnp.concatenate([w_in[l, :, :WIDTH] * q_scale, w_in[l, :, WIDTH:f0], w_in[l, :, f1:]],
                                 axis=1).astype(BF16)
        w_f = jnp.pad(w_in[l, :, f0:f1], ((0, 0), (0, LANES - N_HEADS))).astype(BF16)
        bf_pad = jnp.pad(b_f[l], (0, LANES - N_HEADS)).reshape(1, LANES)

        proj, f = _ln_proj(xr, mod, w_main, w_f, s)
        proj3 = proj.reshape(b, s, MAIN_COLS)
        cqt, fs, vt = _fox_prep(f.reshape(b, s, LANES), bf_pad, proj3)
        o_a = _fox(proj3, fs, vt, cqt)
        o_c = _chunk_attn(proj3, _chunk_bias_table(rel_bias[l], TQ_CHUNK))
        xr = _mix(o_a.reshape(rows, WIDTH), o_c.reshape(rows, WIDTH), proj, xr, mod,
                  w_br_fox[l].astype(BF16), w_br_chunk[l].astype(BF16), w_out[l].astype(BF16),
                  ln1_g[l].reshape(1, d), ln1_b[l].reshape(1, d), s, alpha)

        wa = _chunked_cols(w_up[l, :, :D_FF].astype(BF16), CK_FFN)
        wv = _chunked_cols(w_up[l, :, D_FF:].astype(BF16), CK_FFN)
        cwa = _chunked_cols(conv_w[l, :, :D_FF], CK_FFN)
        cwv = _chunked_cols(conv_w[l, :, D_FF:], CK_FFN)
        cba = conv_b[l, :D_FF].reshape(nck, 1, CK_FFN)
        cbv = conv_b[l, D_FF:].reshape(nck, 1, CK_FFN)
        wd = w_down[l].astype(BF16).reshape(nck, CK_FFN, d)
        x3 = _ffn(xr.reshape(b, s, d), mod, wa, wv, cwa, cwv, cba, cbv, wd,
                  ln2_g[l].reshape(1, d), ln2_b[l].reshape(1, d), alpha)
        xr = x3.reshape(rows, d)
    return xr.reshape(b, s, d)
```

```python
import functools
import math

import numpy as np
import jax
import jax.numpy as jnp
from jax import lax
from jax.experimental import pallas as pl
from jax.experimental.pallas import tpu as pltpu

F32 = jnp.float32
BF16 = jnp.bfloat16

D_MODEL = 1024
HEAD_DIM = 64
N_HEADS = 8
WIDTH = N_HEADS * HEAD_DIM
CHUNK = 64
LEFT_CHUNKS = 8
REL_CLIP = 128
D_FF = 2816
CONV_WIDTH = 3
LN_EPS = 1e-5
N_MOD = 6
LANES = 128
PAIR = 2 * HEAD_DIM
N_PAIRS = N_HEADS // 2
MAIN_COLS = 3 * WIDTH + 3 * WIDTH + 2 * D_MODEL
LOG2E = math.log2(math.e)
NEG = -1e30
VMEM_LIMIT = 56 * 1024 * 1024

TM_PROJ = 1024
TN_PROJ = 1024
T_CUM = 512
TQ_FOX = 512
TK_FOX = 512
TQ_CHUNK = 256
TM_MIX = 512
TM_FFN = 512
CK_FFN = 256
VT_ROWS = HEAD_DIM + 16
HALO = 16


def _ln(x):
    mu = jnp.mean(x, axis=-1, keepdims=True)
    xc = x - mu
    var = jnp.mean(xc * xc, axis=-1, keepdims=True)
    return xc * lax.rsqrt(var + LN_EPS)


def _dot(a, b):
    return jnp.dot(a, b, preferred_element_type=F32)


def _dot_nt(a, b):
    return lax.dot_general(a, b, (((1,), (1,)), ((), ())), preferred_element_type=F32)


def _split3(v):
    hi = v.astype(BF16)
    r = v - hi.astype(F32)
    mid = r.astype(BF16)
    lo = (r - mid.astype(F32)).astype(BF16)
    return hi, mid, lo


def _params(*sem):
    return pltpu.CompilerParams(dimension_semantics=sem, vmem_limit_bytes=VMEM_LIMIT)


def _mod_kernel(c_ref, w_ref, b_ref, o_ref):
    c = c_ref[...]
    cond = c * jax.nn.sigmoid(c)
    a_hi, a_mid, _ = _split3(cond)
    w = w_ref[0]
    w_hi = w.astype(BF16)
    w_lo = (w - w_hi.astype(F32)).astype(BF16)
    acc = _dot(a_hi, w_hi) + _dot(a_mid, w_hi) + _dot(a_hi, w_lo)
    o_ref[0] = acc + b_ref[0]


def _ada_mod(c, w_ada, b_ada):
    depth, d, n = w_ada.shape
    b = c.shape[0]
    rows = 8
    c_pad = jnp.pad(c, ((0, rows - b), (0, 0)))
    tn = 1024
    out = pl.pallas_call(
        _mod_kernel,
        grid=(depth, n // tn),
        in_specs=[
            pl.BlockSpec((rows, d), lambda l, j: (0, 0)),
            pl.BlockSpec((1, d, tn), lambda l, j: (l, 0, j)),
            pl.BlockSpec((1, 1, tn), lambda l, j: (l, 0, j)),
        ],
        out_specs=pl.BlockSpec((1, rows, tn), lambda l, j: (l, 0, j)),
        out_shape=jax.ShapeDtypeStruct((depth, rows, n), F32),
        compiler_params=_params("parallel", "parallel"),
        name="ada_mod",
    )(c_pad, w_ada, b_ada.reshape(depth, 1, n))
    return out[:, :b, :]


def _ln_proj_kernel(x_ref, mod_ref, w_ref, wf_ref, o_ref, f_ref, h_ref):
    @pl.when(pl.program_id(1) == 0)
    def _():
        y = _ln(x_ref[...])
        h = (y * (1.0 + mod_ref[0, 1:2, :]) + mod_ref[0, 0:1, :]).astype(BF16)
        h_ref[...] = h
        f_ref[...] = _dot(h, wf_ref[...])

    o_ref[...] = _dot(h_ref[...], w_ref[...]).astype(BF16)


def _ln_proj(xr, mod, w_main, w_f, seq):
    rows, d = xr.shape
    tm, tn = TM_PROJ, TN_PROJ
    n = w_main.shape[1]
    tiles_per_seq = seq // tm
    return pl.pallas_call(
        _ln_proj_kernel,
        grid=(rows // tm, n // tn),
        in_specs=[
            pl.BlockSpec((tm, d), lambda i, j: (i, 0)),
            pl.BlockSpec((1, N_MOD, d), lambda i, j: (i // tiles_per_seq, 0, 0)),
            pl.BlockSpec((d, tn), lambda i, j: (0, j)),
            pl.BlockSpec((d, LANES), lambda i, j: (0, 0)),
        ],
        out_specs=[
            pl.BlockSpec((tm, tn), lambda i, j: (i, j)),
            pl.BlockSpec((tm, LANES), lambda i, j: (i, 0)),
        ],
        out_shape=[
            jax.ShapeDtypeStruct((rows, n), BF16),
            jax.ShapeDtypeStruct((rows, LANES), F32),
        ],
        scratch_shapes=[pltpu.VMEM((tm, d), BF16)],
        compiler_params=_params("parallel", "arbitrary"),
        name="ln_proj",
    )(xr, mod, w_main, w_f)


def _fox_prep_kernel(f_ref, bf_ref, v_ref, cqt_ref, fs_ref, vt_ref, carry_ref):
    @pl.when(pl.program_id(1) == 0)
    def _():
        carry_ref[...] = jnp.zeros_like(carry_ref)

    z = f_ref[0] + bf_ref[...]
    lf = jnp.minimum(z, 0.0) - jnp.log(1.0 + jnp.exp(-jnp.abs(z)))
    hi, mid, lo = _split3(lf)
    tc = lf.shape[0]
    row = lax.broadcasted_iota(jnp.int32, (tc, tc), 0)
    col = lax.broadcasted_iota(jnp.int32, (tc, tc), 1)
    tri = jnp.where(col <= row, 1.0, 0.0).astype(BF16)
    cum = _dot(tri, hi) + _dot(tri, mid) + _dot(tri, lo) + carry_ref[...]
    carry_ref[...] = cum[tc - 1:tc, :]

    cum2 = cum * LOG2E
    cqt_ref[0] = cum2.T[0:N_HEADS, :]
    neg = -cum2
    n_hi = neg.astype(BF16).astype(F32)
    r1 = neg - n_hi
    n_mid = r1.astype(BF16).astype(F32)
    n_lo = r1 - n_mid
    lane = lax.broadcasted_iota(jnp.int32, (1, LANES), 1)
    placed = jnp.where(lane < N_HEADS, n_hi,
                       jnp.where(lane < 2 * N_HEADS, pltpu.roll(n_mid, N_HEADS, axis=1),
                                 jnp.where(lane < 3 * N_HEADS, pltpu.roll(n_lo, 2 * N_HEADS, axis=1), 0.0)))
    fs_ref[0] = placed.astype(BF16)

    extra = VT_ROWS - HEAD_DIM
    ones_row = jnp.where(lax.broadcasted_iota(jnp.int32, (extra, tc), 0) == 0, 1.0, 0.0)
    for p in range(N_PAIRS):
        vt = v_ref[0, :, p * PAIR:(p + 1) * PAIR].astype(F32).T
        vt_ref[0, 2 * p] = jnp.concatenate([vt[0:HEAD_DIM], ones_row], axis=0).astype(BF16)
        vt_ref[0, 2 * p + 1] = jnp.concatenate([vt[HEAD_DIM:], ones_row], axis=0).astype(BF16)


def _fox_prep(f3, bf_pad, proj3):
    b, s, _ = f3.shape
    tc = T_CUM
    v_blk = (2 * WIDTH) // WIDTH
    return pl.pallas_call(
        _fox_prep_kernel,
        grid=(b, s // tc),
        in_specs=[
            pl.BlockSpec((1, tc, LANES), lambda bi, t: (bi, t, 0)),
            pl.BlockSpec((1, LANES), lambda bi, t: (0, 0)),
            pl.BlockSpec((1, tc, WIDTH), lambda bi, t: (bi, t, v_blk)),
        ],
        out_specs=[
            pl.BlockSpec((1, N_HEADS, tc), lambda bi, t: (bi, 0, t)),
            pl.BlockSpec((1, tc, LANES), lambda bi, t: (bi, t, 0)),
            pl.BlockSpec((1, N_HEADS, VT_ROWS, tc), lambda bi, t: (bi, 0, 0, t)),
        ],
        out_shape=[
            jax.ShapeDtypeStruct((b, N_HEADS, s), F32),
            jax.ShapeDtypeStruct((b, s, LANES), BF16),
            jax.ShapeDtypeStruct((b, N_HEADS, VT_ROWS, s), BF16),
        ],
        scratch_shapes=[pltpu.VMEM((1, LANES), F32)],
        compiler_params=_params("parallel", "arbitrary"),
        name="fox_prep",
    )(f3, bf_pad, proj3)


def _fox_kernel(q_ref, k_ref, fs_ref, vt_ref, cqt_ref, o_ref, qt_ref, sta_ref, stb_ref, m_ref, acc_ref,
                *, tq, tk):
    p = pl.program_id(1)
    i = pl.program_id(2)
    q0 = i * tq
    n_full = q0 // tk
    qt = q_ref[0].astype(F32).T
    row = lax.broadcasted_iota(jnp.int32, (PAIR, 1), 0)
    qts, cqs = [], []
    for hh in range(2):
        h = 2 * p + hh
        in_head = (row < HEAD_DIM) if hh == 0 else (row >= HEAD_DIM)
        q_rows = jnp.where(in_head, qt, 0.0)
        pick = (row == h) | (row == h + N_HEADS) | (row == h + 2 * N_HEADS)
        one_rows = jnp.broadcast_to(jnp.where(pick, 1.0, 0.0), (PAIR, tq))
        qts.append(jnp.concatenate([q_rows, one_rows], axis=0).astype(BF16))
        cqs.append(cqt_ref[0, pl.ds(h, 1), :])
    qt_both = jnp.concatenate(qts, axis=1)
    cq = jnp.concatenate(cqs, axis=1)
    key_off = lax.broadcasted_iota(jnp.int32, (tk, 1), 0)
    q_pos = q0 + lax.broadcasted_iota(jnp.int32, (1, 2 * tq), 1) % tq

    qt_ref[...] = qt_both
    m_ref[...] = jnp.full((1, 2 * tq), NEG, F32)
    acc_ref[...] = jnp.zeros((2, VT_ROWS, tq), F32)

    def qk(j, st_ref):
        ks = pl.multiple_of(j * tk, tk)
        kf = jnp.concatenate([k_ref[0, pl.ds(ks, tk), :], fs_ref[0, pl.ds(ks, tk), :]], axis=1)
        st_ref[...] = _dot(kf, qt_ref[...])

    def soft_pv(j, st_ref, masked):
        ks = pl.multiple_of(j * tk, tk)
        st = st_ref[...]
        if masked:
            st = jnp.where(key_off + ks <= q_pos, st, NEG)
        m = m_ref[...]
        m_new = jnp.maximum(m, jnp.max(st, axis=0, keepdims=True) + cq)
        alpha = jnp.exp2(m - m_new)
        pb = jnp.exp2(st - (m_new - cq)).astype(BF16)
        m_ref[...] = m_new
        acc_ref[0] = alpha[:, :tq] * acc_ref[0] + _dot(vt_ref[0, 0, :, pl.ds(ks, tk)], pb[:, :tq])
        acc_ref[1] = alpha[:, tq:] * acc_ref[1] + _dot(vt_ref[0, 1, :, pl.ds(ks, tk)], pb[:, tq:])

    qk(0, sta_ref)

    def pair(jj, c):
        qk(2 * jj + 1, stb_ref)
        soft_pv(2 * jj, sta_ref, False)
        qk(2 * jj + 2, sta_ref)
        soft_pv(2 * jj + 1, stb_ref, False)
        return c

    lax.fori_loop(0, n_full // 2, pair, 0)

    @pl.when(n_full % 2 == 1)
    def _():
        qk(n_full, stb_ref)
        soft_pv(n_full - 1, sta_ref, False)
        soft_pv(n_full, stb_ref, True)

    @pl.when(n_full % 2 == 0)
    def _():
        soft_pv(n_full, sta_ref, True)

    ot = jnp.concatenate([acc_ref[hh, 0:HEAD_DIM, :] * (1.0 / acc_ref[hh, HEAD_DIM:HEAD_DIM + 1, :])
                          for hh in range(2)], axis=0)
    o_ref[0] = ot.T.astype(BF16)


def _fox(proj3, fs, vt, cqt):
    b, s, _ = proj3.shape
    tq, tk = TQ_FOX, TK_FOX
    assert tk % tq == 0 and s % tk == 0
    return pl.pallas_call(
        functools.partial(_fox_kernel, tq=tq, tk=tk),
        grid=(b, N_PAIRS, s // tq),
        in_specs=[
            pl.BlockSpec((1, tq, PAIR), lambda bi, p, i: (bi, i, p)),
            pl.BlockSpec((1, s, PAIR), lambda bi, p, i: (bi, 0, N_PAIRS + p)),
            pl.BlockSpec((1, s, LANES), lambda bi, p, i: (bi, 0, 0)),
            pl.BlockSpec((1, 2, VT_ROWS, s), lambda bi, p, i: (bi, p, 0, 0)),
            pl.BlockSpec((1, N_HEADS, tq), lambda bi, p, i: (bi, 0, i)),
        ],
        out_specs=pl.BlockSpec((1, tq, PAIR), lambda bi, p, i: (bi, i, p)),
        out_shape=jax.ShapeDtypeStruct((b, s, WIDTH), BF16),
        scratch_shapes=[pltpu.VMEM((2 * PAIR, 2 * tq), BF16), pltpu.VMEM((tk, 2 * tq), F32),
                        pltpu.VMEM((tk, 2 * tq), F32), pltpu.VMEM((1, 2 * tq), F32),
                        pltpu.VMEM((2, VT_ROWS, tq), F32)],
        compiler_params=_params("parallel", "parallel", "arbitrary"),
        name="fox_attn",
    )(proj3, proj3, fs, vt, cqt)


def _chunk_kernel(q_ref, k_ref, v_ref, bias_ref, o_ref, *, tq, nsub):
    i = pl.program_id(2)
    lane = lax.broadcasted_iota(jnp.int32, (1, PAIR), 1)
    q = q_ref[0]
    scale = 1.0 / math.sqrt(HEAD_DIM)
    outs = []
    for hh in range(2):
        in_head = (lane < HEAD_DIM) if hh == 0 else (lane >= HEAD_DIM)
        qm = (jnp.where(in_head, q, jnp.zeros_like(q)).astype(F32) * scale).astype(BF16)
        logits, starts = [], []
        for sub in range(nsub):
            blk = i - (nsub - 1) + sub
            start = pl.multiple_of(jnp.maximum(blk, 0) * tq, tq)
            kj = k_ref[0, pl.ds(start, tq), :]
            s = _dot_nt(qm, kj) + bias_ref[hh, :, sub * tq:(sub + 1) * tq]
            logits.append(jnp.where(blk >= 0, s, NEG))
            starts.append(start)
        m = functools.reduce(jnp.maximum, [jnp.max(s, axis=1, keepdims=True) for s in logits])
        l = jnp.zeros((tq, 1), F32)
        acc = jnp.zeros((tq, PAIR), F32)
        for s, start in zip(logits, starts):
            pexp = jnp.exp(s - m)
            l = l + jnp.sum(pexp, axis=1, keepdims=True)
            acc = acc + _dot(pexp.astype(BF16), v_ref[0, pl.ds(start, tq), :])
        outs.append(acc / l)
    o_ref[0] = jnp.where(lane < HEAD_DIM, outs[0], outs[1]).astype(BF16)


def _chunk_bias_table(rel_table, tq):
    left = LEFT_CHUNKS * CHUNK
    w = tq + left
    period = tq + w - 1
    heads = rel_table.shape[0]
    top = rel_table[:, 2 * REL_CLIP:]
    bot = rel_table[:, :1]
    n_top = left - REL_CLIP + 1
    n_bot = w - n_top - 2 * REL_CLIP
    u = jnp.concatenate([
        jnp.broadcast_to(top, (heads, n_top)),
        rel_table[:, :2 * REL_CLIP][:, ::-1],
        jnp.broadcast_to(bot, (heads, n_bot)),
        jnp.broadcast_to(top, (heads, tq - 1)),
    ], axis=1).astype(F32)
    flat = jnp.tile(u, (1, tq + 1))[:, :tq * (period - 1)]
    bias = flat.reshape(heads, tq, period - 1)[:, :, :w]
    qc = np.arange(tq)[:, None] // CHUNK
    kc = np.arange(w)[None, :] // CHUNK
    band = (kc >= qc) & (kc <= qc + LEFT_CHUNKS)
    return jnp.where(jnp.asarray(band)[None], bias, NEG)


def _chunk_attn(proj3, bias_table):
    b, s, _ = proj3.shape
    tq = TQ_CHUNK
    left = LEFT_CHUNKS * CHUNK
    assert left % tq == 0 and tq % CHUNK == 0
    nsub = left // tq + 1
    col0 = 3 * N_PAIRS
    return pl.pallas_call(
        functools.partial(_chunk_kernel, tq=tq, nsub=nsub),
        grid=(b, N_PAIRS, s // tq),
        in_specs=[
            pl.BlockSpec((1, tq, PAIR), lambda bi, p, i: (bi, i, col0 + p)),
            pl.BlockSpec((1, s, PAIR), lambda bi, p, i: (bi, 0, col0 + N_PAIRS + p)),
            pl.BlockSpec((1, s, PAIR), lambda bi, p, i: (bi, 0, col0 + 2 * N_PAIRS + p)),
            pl.BlockSpec((2, tq, tq + left), lambda bi, p, i: (p, 0, 0)),
        ],
        out_specs=pl.BlockSpec((1, tq, PAIR), lambda bi, p, i: (bi, i, p)),
        out_shape=jax.ShapeDtypeStruct((b, s, WIDTH), BF16),
        compiler_params=_params("parallel", "parallel", "arbitrary"),
        name="chunk_attn",
    )(proj3, proj3, proj3, bias_table)


def _mix_kernel(oa_ref, oc_ref, ga_ref, gc_ref, x_ref, mod_ref, wa_ref, wc_ref, wo_ref,
                g_ref, b_ref, o_ref, *, alpha):
    ya = _dot(oa_ref[...], wa_ref[...])
    yc = _dot(oc_ref[...], wc_ref[...])
    merged = (jax.nn.sigmoid(ga_ref[...].astype(F32)) * ya
              + jax.nn.sigmoid(gc_ref[...].astype(F32)) * yc)
    mix = _dot(merged.astype(BF16), wo_ref[...])
    z = alpha * x_ref[...] + (1.0 + mod_ref[0, 2:3, :]) * mix
    o_ref[...] = _ln(z) * g_ref[...] + b_ref[...]


def _mix(oa, oc, proj, xr, mod, w_a, w_c, w_o, gain, bias, seq, alpha):
    rows, d = xr.shape
    tm = TM_MIX
    tiles_per_seq = seq // tm
    gate_blk = (2 * 3 * WIDTH) // d
    const = lambda i: (0, 0)
    return pl.pallas_call(
        functools.partial(_mix_kernel, alpha=alpha),
        grid=(rows // tm,),
        in_specs=[
            pl.BlockSpec((tm, WIDTH), lambda i: (i, 0)),
            pl.BlockSpec((tm, WIDTH), lambda i: (i, 0)),
            pl.BlockSpec((tm, d), lambda i: (i, gate_blk)),
            pl.BlockSpec((tm, d), lambda i: (i, gate_blk + 1)),
            pl.BlockSpec((tm, d), lambda i: (i, 0)),
            pl.BlockSpec((1, N_MOD, d), lambda i: (i // tiles_per_seq, 0, 0)),
            pl.BlockSpec((WIDTH, d), const),
            pl.BlockSpec((WIDTH, d), const),
            pl.BlockSpec((d, d), const),
            pl.BlockSpec((1, d), const),
            pl.BlockSpec((1, d), const),
        ],
        out_specs=pl.BlockSpec((tm, d), lambda i: (i, 0)),
        out_shape=jax.ShapeDtypeStruct((rows, d), F32),
        compiler_params=_params("parallel"),
        name="mix_out",
    )(oa, oc, proj, proj, xr, mod, w_a, w_c, w_o, gain, bias)


def _ffn_kernel(x_ref, xh_ref, mod_ref, wa_ref, wv_ref, cwa_ref, cwv_ref, cba_ref, cbv_ref,
                wd_ref, g_ref, b_ref, o_ref, h_ref, acc_ref, *, alpha, tm, nck):
    t = pl.program_id(1)
    x = x_ref[0]
    sc = 1.0 + mod_ref[0, 4:5, :]
    sh = mod_ref[0, 3:4, :]
    h_ref[HALO:, :] = (_ln(x) * sc + sh).astype(BF16)
    keep = jnp.where(t > 0, 1.0, 0.0)
    h_ref[0:HALO, :] = ((_ln(xh_ref[0]) * sc + sh) * keep).astype(BF16)
    acc_ref[...] = jnp.zeros_like(acc_ref)

    def conv(u, cw, cb):
        lo = HALO - (CONV_WIDTH - 1)
        y = cb
        for j in range(CONV_WIDTH):
            y = y + cw[j:j + 1, :] * u[lo + j:lo + j + tm, :]
        return y

    def up(c):
        he = h_ref[...]
        return _dot(he, wa_ref[c]), _dot(he, wv_ref[c])

    u = up(0)
    for c in range(nck):
        u_next = up(c + 1) if c + 1 < nck else None
        a = conv(u[0], cwa_ref[c], cba_ref[c])
        v = conv(u[1], cwv_ref[c], cbv_ref[c])
        act = (a * jax.nn.sigmoid(a) * v).astype(BF16)
        acc_ref[...] += _dot(act, wd_ref[c])
        u = u_next
    z = alpha * x + (1.0 + mod_ref[0, 5:6, :]) * acc_ref[...]
    o_ref[0] = _ln(z) * g_ref[...] + b_ref[...]


def _ffn(x3, mod, wa, wv, cwa, cwv, cba, cbv, wd, gain, bias, alpha):
    b, s, d = x3.shape
    tm = TM_FFN
    nck, _, ck = wa.shape
    halo_per_tile = tm // HALO
    c3 = lambda bi, t: (0, 0, 0)
    c2 = lambda bi, t: (0, 0)
    return pl.pallas_call(
        functools.partial(_ffn_kernel, alpha=alpha, tm=tm, nck=nck),
        grid=(b, s // tm),
        in_specs=[
            pl.BlockSpec((1, tm, d), lambda bi, t: (bi, t, 0)),
            pl.BlockSpec((1, HALO, d), lambda bi, t: (bi, jnp.maximum(t * halo_per_tile - 1, 0), 0)),
            pl.BlockSpec((1, N_MOD, d), lambda bi, t: (bi, 0, 0)),
            pl.BlockSpec((nck, d, ck), c3),
            pl.BlockSpec((nck, d, ck), c3),
            pl.BlockSpec((nck, CONV_WIDTH, ck), c3),
            pl.BlockSpec((nck, CONV_WIDTH, ck), c3),
            pl.BlockSpec((nck, 1, ck), c3),
            pl.BlockSpec((nck, 1, ck), c3),
            pl.BlockSpec((nck, ck, d), c3),
            pl.BlockSpec((1, d), c2),
            pl.BlockSpec((1, d), c2),
        ],
        out_specs=pl.BlockSpec((1, tm, d), lambda bi, t: (bi, t, 0)),
        out_shape=jax.ShapeDtypeStruct((b, s, d), F32),
        scratch_shapes=[pltpu.VMEM((tm + HALO, d), BF16), pltpu.VMEM((tm, d), F32)],
        compiler_params=_params("parallel", "parallel"),
        name="conv_ffn",
    )(x3, x3, mod, wa, wv, cwa, cwv, cba, cbv, wd, gain, bias)


def _chunked_cols(w, ck):
    k, n = w.shape
    return w.reshape(k, n // ck, ck).transpose(1, 0, 2)


def kernel(x, c, w_in, b_f, rel_bias, w_br_fox, w_br_chunk, w_out, w_up, conv_w, conv_b, w_down,
           w_ada, b_ada, ln1_g, ln1_b, ln2_g, ln2_b):
    b, s, d = x.shape
    depth = w_in.shape[0]
    alpha = (2.0 * depth) ** 0.25
    rows = b * s
    mod_all = _ada_mod(c, w_ada, b_ada).reshape(depth, b, N_MOD, d)

    f0 = 3 * WIDTH
    f1 = f0 + N_HEADS
    nck = D_FF // CK_FFN
    xr = x.reshape(rows, d)
    for l in range(depth):
        mod = mod_all[l]
        q_scale = LOG2E / math.sqrt(HEAD_DIM)
        w_main = jnp.concatenate([w_in[l, :, :WIDTH] * q_scale, w_in[l, :, WIDTH:f0], w_in[l, :, f1:]],
                                 axis=1).astype(BF16)
        w_f = jnp.pad(w_in[l, :, f0:f1], ((0, 0), (0, LANES - N_HEADS))).astype(BF16)
        bf_pad = jnp.pad(b_f[l], (0, LANES - N_HEADS)).reshape(1, LANES)

        proj, f = _ln_proj(xr, mod, w_main, w_f, s)
        proj3 = proj.reshape(b, s, MAIN_COLS)
        cqt, fs, vt = _fox_prep(f.reshape(b, s, LANES), bf_pad, proj3)
        o_a = _fox(proj3, fs, vt, cqt)
        o_c = _chunk_attn(proj3, _chunk_bias_table(rel_bias[l], TQ_CHUNK))
        xr = _mix(o_a.reshape(rows, WIDTH), o_c.reshape(rows, WIDTH), proj, xr, mod,
                  w_br_fox[l].astype(BF16), w_br_chunk[l].astype(BF16), w_out[l].astype(BF16),
                  ln1_g[l].reshape(1, d), ln1_b[l].reshape(1, d), s, alpha)

        wa = _chunked_cols(w_up[l, :, :D_FF].astype(BF16), CK_FFN)
        wv = _chunked_cols(w_up[l, :, D_FF:].astype(BF16), CK_FFN)
        cwa = _chunked_cols(conv_w[l, :, :D_FF], CK_FFN)
        cwv = _chunked_cols(conv_w[l, :, D_FF:], CK_FFN)
        cba = conv_b[l, :D_FF].reshape(nck, 1, CK_FFN)
        cbv = conv_b[l, D_FF:].reshape(nck, 1, CK_FFN)
        wd = w_down[l].astype(BF16).reshape(nck, CK_FFN, d)
        x3 = _ffn(xr.reshape(b, s, d), mod, wa, wv, cwa, cwv, cba, cbv, wd,
                  ln2_g[l].reshape(1, d), ln2_b[l].reshape(1, d), alpha)
        xr = x3.reshape(rows, d)
    return xr.reshape(b, s, d)
```

```python
import functools
import math

import numpy as np
import jax
import jax.numpy as jnp
from jax import lax
from jax.experimental import pallas as pl
from jax.experimental.pallas import tpu as pltpu

F32 = jnp.float32
BF16 = jnp.bfloat16

D_MODEL = 1024
HEAD_DIM = 64
N_HEADS = 8
WIDTH = N_HEADS * HEAD_DIM
CHUNK = 64
LEFT_CHUNKS = 8
REL_CLIP = 128
CONV_WIDTH = 3
LN_EPS = 1e-5
N_MOD = 6
LANES = 128
PAIR = 2 * HEAD_DIM
N_PAIRS = N_HEADS // 2
MAIN_COLS = 3 * WIDTH + 3 * WIDTH + 2 * D_MODEL
LOG2E = math.log2(math.e)
NEG = -1e30
VMEM_LIMIT = 56 * 1024 * 1024

TM_PROJ = 1024
TN_PROJ = 1024
T_CUM = 512
TQ_FOX = 512
TK_FOX = 512
TQ_CHUNK = 256
NSUB_CHUNK = 4
TM_MIX = 512
TM_FFN = 512
CK_FFN = 256
VT_ROWS = HEAD_DIM + 16
HALO = 16


def _ln(x):
    mu = jnp.mean(x, axis=-1, keepdims=True)
    xc = x - mu
    var = jnp.mean(xc * xc, axis=-1, keepdims=True)
    return xc * lax.rsqrt(var + LN_EPS)


def _dot(a, b):
    return jnp.dot(a, b, preferred_element_type=F32)


def _split3(v):
    hi = v.astype(BF16)
    r = v - hi.astype(F32)
    mid = r.astype(BF16)
    lo = (r - mid.astype(F32)).astype(BF16)
    return hi, mid, lo


def _params(*sem):
    return pltpu.CompilerParams(dimension_semantics=sem, vmem_limit_bytes=VMEM_LIMIT)


def _mod_kernel(c_ref, w_ref, b_ref, o_ref):
    c = c_ref[...]
    cond = c * jax.nn.sigmoid(c)
    a_hi, a_mid, _ = _split3(cond)
    w = w_ref[0]
    w_hi = w.astype(BF16)
    w_lo = (w - w_hi.astype(F32)).astype(BF16)
    acc = _dot(a_hi, w_hi) + _dot(a_mid, w_hi) + _dot(a_hi, w_lo)
    o_ref[0] = acc + b_ref[0]


def _ada_mod(c, w_ada, b_ada):
    depth, d, n = w_ada.shape
    b = c.shape[0]
    rows = 8
    c_pad = jnp.pad(c, ((0, rows - b), (0, 0)))
    tn = 1024
    out = pl.pallas_call(
        _mod_kernel,
        grid=(depth, n // tn),
        in_specs=[
            pl.BlockSpec((rows, d), lambda l, j: (0, 0)),
            pl.BlockSpec((1, d, tn), lambda l, j: (l, 0, j)),
            pl.BlockSpec((1, 1, tn), lambda l, j: (l, 0, j)),
        ],
        out_specs=pl.BlockSpec((1, rows, tn), lambda l, j: (l, 0, j)),
        out_shape=jax.ShapeDtypeStruct((depth, rows, n), F32),
        compiler_params=_params("parallel", "parallel"),
        name="ada_mod",
    )(c_pad, w_ada, b_ada.reshape(depth, 1, n))
    return out[:, :b, :]


def _ln_proj_kernel(x_ref, mod_ref, w_ref, wf_ref, o_ref, f_ref, h_ref):
    @pl.when(pl.program_id(1) == 0)
    def _():
        y = _ln(x_ref[...])
        h = (y * (1.0 + mod_ref[0, 1:2, :]) + mod_ref[0, 0:1, :]).astype(BF16)
        h_ref[...] = h
        f_ref[...] = _dot(h, wf_ref[...])

    o_ref[...] = _dot(h_ref[...], w_ref[...]).astype(BF16)


def _ln_proj(xr, mod, w_main, w_f, seq):
    rows, d = xr.shape
    tm, tn = TM_PROJ, TN_PROJ
    n = w_main.shape[1]
    tiles_per_seq = seq // tm
    return pl.pallas_call(
        _ln_proj_kernel,
        grid=(rows // tm, n // tn),
        in_specs=[
            pl.BlockSpec((tm, d), lambda i, j: (i, 0)),
            pl.BlockSpec((1, N_MOD, d), lambda i, j: (i // tiles_per_seq, 0, 0)),
            pl.BlockSpec((d, tn), lambda i, j: (0, j)),
            pl.BlockSpec((d, LANES), lambda i, j: (0, 0)),
        ],
        out_specs=[
            pl.BlockSpec((tm, tn), lambda i, j: (i, j)),
            pl.BlockSpec((tm, LANES), lambda i, j: (i, 0)),
        ],
        out_shape=[
            jax.ShapeDtypeStruct((rows, n), BF16),
            jax.ShapeDtypeStruct((rows, LANES), F32),
        ],
        scratch_shapes=[pltpu.VMEM((tm, d), BF16)],
        compiler_params=_params("parallel", "arbitrary"),
        name="ln_proj",
    )(xr, mod, w_main, w_f)


def _attn_prep_kernel(f_ref, bf_ref, v_ref, vc_ref, cqt_ref, fs_ref, vt_ref, vtc_ref, carry_ref):
    @pl.when(pl.program_id(1) == 0)
    def _():
        carry_ref[...] = jnp.zeros_like(carry_ref)

    z = f_ref[0] + bf_ref[...]
    lf = jnp.minimum(z, 0.0) - jnp.log(1.0 + jnp.exp(-jnp.abs(z)))
    hi, mid, lo = _split3(lf)
    tc = lf.shape[0]
    row = lax.broadcasted_iota(jnp.int32, (tc, tc), 0)
    col = lax.broadcasted_iota(jnp.int32, (tc, tc), 1)
    tri = jnp.where(col <= row, 1.0, 0.0).astype(BF16)
    cum = _dot(tri, hi) + _dot(tri, mid) + _dot(tri, lo) + carry_ref[...]
    carry_ref[...] = cum[tc - 1:tc, :]

    cum2 = cum * LOG2E
    cqt_ref[0] = cum2.T[0:N_HEADS, :]
    neg = -cum2
    n_hi = neg.astype(BF16).astype(F32)
    r1 = neg - n_hi
    n_mid = r1.astype(BF16).astype(F32)
    n_lo = r1 - n_mid
    lane = lax.broadcasted_iota(jnp.int32, (1, LANES), 1)
    placed = jnp.where(lane < N_HEADS, n_hi,
                       jnp.where(lane < 2 * N_HEADS, pltpu.roll(n_mid, N_HEADS, axis=1),
                                 jnp.where(lane < 3 * N_HEADS, pltpu.roll(n_lo, 2 * N_HEADS, axis=1), 0.0)))
    fs_ref[0] = placed.astype(BF16)

    extra = VT_ROWS - HEAD_DIM
    ones_row = jnp.where(lax.broadcasted_iota(jnp.int32, (extra, tc), 0) == 0, 1.0, 0.0)
    for src_ref, dst_ref in ((v_ref, vt_ref), (vc_ref, vtc_ref)):
        for p in range(N_PAIRS):
            vt = src_ref[0, :, p * PAIR:(p + 1) * PAIR].astype(F32).T
            dst_ref[0, 2 * p] = jnp.concatenate([vt[0:HEAD_DIM], ones_row], axis=0).astype(BF16)
            dst_ref[0, 2 * p + 1] = jnp.concatenate([vt[HEAD_DIM:], ones_row], axis=0).astype(BF16)


def _attn_prep(f3, bf_pad, proj3):
    b, s, _ = f3.shape
    tc = T_CUM
    v_blk = 2
    vc_blk = 5
    return pl.pallas_call(
        _attn_prep_kernel,
        grid=(b, s // tc),
        in_specs=[
            pl.BlockSpec((1, tc, LANES), lambda bi, t: (bi, t, 0)),
            pl.BlockSpec((1, LANES), lambda bi, t: (0, 0)),
            pl.BlockSpec((1, tc, WIDTH), lambda bi, t: (bi, t, v_blk)),
            pl.BlockSpec((1, tc, WIDTH), lambda bi, t: (bi, t, vc_blk)),
        ],
        out_specs=[
            pl.BlockSpec((1, N_HEADS, tc), lambda bi, t: (bi, 0, t)),
            pl.BlockSpec((1, tc, LANES), lambda bi, t: (bi, t, 0)),
            pl.BlockSpec((1, N_HEADS, VT_ROWS, tc), lambda bi, t: (bi, 0, 0, t)),
            pl.BlockSpec((1, N_HEADS, VT_ROWS, tc), lambda bi, t: (bi, 0, 0, t)),
        ],
        out_shape=[
            jax.ShapeDtypeStruct((b, N_HEADS, s), F32),
            jax.ShapeDtypeStruct((b, s, LANES), BF16),
            jax.ShapeDtypeStruct((b, N_HEADS, VT_ROWS, s), BF16),
            jax.ShapeDtypeStruct((b, N_HEADS, VT_ROWS, s), BF16),
        ],
        scratch_shapes=[pltpu.VMEM((1, LANES), F32)],
        compiler_params=_params("parallel", "arbitrary"),
        name="attn_prep",
    )(f3, bf_pad, proj3, proj3)


def _fox_kernel(q_ref, k_ref, fs_ref, vt_ref, cqt_ref, o_ref, qt_ref, sta_ref, stb_ref, m_ref, acc_ref,
                *, tq, tk):
    p = pl.program_id(1)
    i = pl.program_id(2)
    q0 = i * tq
    n_full = q0 // tk
    qt = q_ref[0].astype(F32).T
    row = lax.broadcasted_iota(jnp.int32, (PAIR, 1), 0)
    qts, cqs = [], []
    for hh in range(2):
        h = 2 * p + hh
        in_head = (row < HEAD_DIM) if hh == 0 else (row >= HEAD_DIM)
        q_rows = jnp.where(in_head, qt, 0.0)
        pick = (row == h) | (row == h + N_HEADS) | (row == h + 2 * N_HEADS)
        one_rows = jnp.broadcast_to(jnp.where(pick, 1.0, 0.0), (PAIR, tq))
        qts.append(jnp.concatenate([q_rows, one_rows], axis=0).astype(BF16))
        cqs.append(cqt_ref[0, pl.ds(h, 1), :])
    qt_both = jnp.concatenate(qts, axis=1)
    cq = jnp.concatenate(cqs, axis=1)
    key_off = lax.broadcasted_iota(jnp.int32, (tk, 1), 0)
    q_pos = q0 + lax.broadcasted_iota(jnp.int32, (1, 2 * tq), 1) % tq

    qt_ref[...] = qt_both
    m_ref[...] = jnp.full((1, 2 * tq), NEG, F32)
    acc_ref[...] = jnp.zeros((2, VT_ROWS, tq), F32)

    def qk(j, st_ref):
        ks = pl.multiple_of(j * tk, tk)
        kf = jnp.concatenate([k_ref[0, pl.ds(ks, tk), :], fs_ref[0, pl.ds(ks, tk), :]], axis=1)
        st_ref[...] = _dot(kf, qt_ref[...])

    def soft_pv(j, st_ref, masked):
        ks = pl.multiple_of(j * tk, tk)
        st = st_ref[...]
        if masked:
            st = jnp.where(key_off + ks <= q_pos, st, NEG)
        m = m_ref[...]
        m_new = jnp.maximum(m, jnp.max(st, axis=0, keepdims=True) + cq)
        alpha = jnp.exp2(m - m_new)
        pb = jnp.exp2(st - (m_new - cq)).astype(BF16)
        m_ref[...] = m_new
        acc_ref[0] = alpha[:, :tq] * acc_ref[0] + _dot(vt_ref[0, 0, :, pl.ds(ks, tk)], pb[:, :tq])
        acc_ref[1] = alpha[:, tq:] * acc_ref[1] + _dot(vt_ref[0, 1, :, pl.ds(ks, tk)], pb[:, tq:])

    qk(0, sta_ref)

    def pair(jj, c):
        qk(2 * jj + 1, stb_ref)
        soft_pv(2 * jj, sta_ref, False)
        qk(2 * jj + 2, sta_ref)
        soft_pv(2 * jj + 1, stb_ref, False)
        return c

    lax.fori_loop(0, n_full // 2, pair, 0)

    @pl.when(n_full % 2 == 1)
    def _():
        qk(n_full, stb_ref)
        soft_pv(n_full - 1, sta_ref, False)
        soft_pv(n_full, stb_ref, True)

    @pl.when(n_full % 2 == 0)
    def _():
        soft_pv(n_full, sta_ref, True)

    ot = jnp.concatenate([acc_ref[hh, 0:HEAD_DIM, :] * (1.0 / acc_ref[hh, HEAD_DIM:HEAD_DIM + 1, :])
                          for hh in range(2)], axis=0)
    o_ref[0] = ot.T.astype(BF16)


def _fox(proj3, fs, vt, cqt):
    b, s, _ = proj3.shape
    tq, tk = TQ_FOX, TK_FOX
    assert tk % tq == 0 and s % tk == 0
    return pl.pallas_call(
        functools.partial(_fox_kernel, tq=tq, tk=tk),
        grid=(b, N_PAIRS, s // tq),
        in_specs=[
            pl.BlockSpec((1, tq, PAIR), lambda bi, p, i: (bi, i, p)),
            pl.BlockSpec((1, s, PAIR), lambda bi, p, i: (bi, 0, N_PAIRS + p)),
            pl.BlockSpec((1, s, LANES), lambda bi, p, i: (bi, 0, 0)),
            pl.BlockSpec((1, 2, VT_ROWS, s), lambda bi, p, i: (bi, p, 0, 0)),
            pl.BlockSpec((1, N_HEADS, tq), lambda bi, p, i: (bi, 0, i)),
        ],
        out_specs=pl.BlockSpec((1, tq, PAIR), lambda bi, p, i: (bi, i, p)),
        out_shape=jax.ShapeDtypeStruct((b, s, WIDTH), BF16),
        scratch_shapes=[pltpu.VMEM((2 * PAIR, 2 * tq), BF16), pltpu.VMEM((tk, 2 * tq), F32),
                        pltpu.VMEM((tk, 2 * tq), F32), pltpu.VMEM((1, 2 * tq), F32),
                        pltpu.VMEM((2, VT_ROWS, tq), F32)],
        compiler_params=_params("parallel", "parallel", "arbitrary"),
        name="fox_attn",
    )(proj3, proj3, fs, vt, cqt)


def _chunk_kernel(q_ref, k_ref, vt_ref, bias_a_ref, bias_b_ref, bias_ref, o_ref, *, tq, nblk, nsub):
    i = pl.program_id(2)
    w = nblk * tq
    qt = q_ref[0].astype(F32).T
    row = lax.broadcasted_iota(jnp.int32, (PAIR, 1), 0)
    head_rows = [row < HEAD_DIM, row >= HEAD_DIM]
    bias_refs = [bias_a_ref, bias_b_ref] + [bias_ref] * (nsub - 2)

    def scores(sb):
        g = nsub * i + sb
        ws = pl.multiple_of(jnp.maximum(g - (nblk - 1), 0) * tq, tq)
        qs = qt[:, sb * tq:(sb + 1) * tq]
        qt_both = jnp.concatenate([jnp.where(hr, qs, 0.0) for hr in head_rows], axis=1).astype(BF16)
        st = _dot(k_ref[0, pl.ds(ws, w), :], qt_both)
        return ws, [st[:, hh * tq:(hh + 1) * tq] + bias_refs[sb][0, hh] for hh in range(2)]

    def finish(sb, ws, sts):
        cols = []
        for hh in range(2):
            st = sts[hh]
            pb = jnp.exp2(st - jnp.max(st, axis=0, keepdims=True)).astype(BF16)
            acc = _dot(vt_ref[0, hh, :, pl.ds(ws, w)], pb)
            cols.append(acc[0:HEAD_DIM] * (1.0 / acc[HEAD_DIM:HEAD_DIM + 1]))
        o_ref[0, sb * tq:(sb + 1) * tq, :] = jnp.concatenate(cols, axis=0).T.astype(BF16)

    pending = scores(0)
    for sb in range(nsub):
        nxt = scores(sb + 1) if sb + 1 < nsub else None
        finish(sb, *pending)
        pending = nxt


def _chunk_bias_tables(rel_table, tq):
    left = LEFT_CHUNKS * CHUNK
    w = tq + left
    period = tq + w - 1
    heads = rel_table.shape[0]
    top = rel_table[:, 2 * REL_CLIP:]
    bot = rel_table[:, :1]
    n_top = left - REL_CLIP + 1
    n_bot = w - n_top - 2 * REL_CLIP
    u = jnp.concatenate([
        jnp.broadcast_to(top, (heads, n_top)),
        rel_table[:, :2 * REL_CLIP][:, ::-1],
        jnp.broadcast_to(bot, (heads, n_bot)),
        jnp.broadcast_to(top, (heads, tq - 1)),
    ], axis=1).astype(F32)
    flat = jnp.tile(u, (1, tq + 1))[:, :tq * (period - 1)]
    bias = flat.reshape(heads, tq, period - 1)[:, :, :w]
    qc = np.arange(tq)[:, None] // CHUNK
    kc = np.arange(w)[None, :] // CHUNK
    band = (kc >= qc) & (kc <= qc + LEFT_CHUNKS)
    full = jnp.where(jnp.asarray(band)[None], bias * LOG2E, NEG).transpose(0, 2, 1)
    gone = jnp.full((heads, tq, tq), NEG, F32)
    v1 = jnp.concatenate([full[:, tq:], gone], axis=1)
    v0 = jnp.concatenate([full[:, 2 * tq:], gone, gone], axis=1)
    return jnp.stack([v0, v1, full])


def _chunk_attn(proj3, vtc, bias_tables):
    b, s, _ = proj3.shape
    tq = TQ_CHUNK
    left = LEFT_CHUNKS * CHUNK
    assert left == 2 * tq and s >= 3 * tq
    nblk = 3
    nsub = NSUB_CHUNK
    assert nsub >= 2 and s % (nsub * tq) == 0
    col0 = 3 * N_PAIRS
    return pl.pallas_call(
        functools.partial(_chunk_kernel, tq=tq, nblk=nblk, nsub=nsub),
        grid=(b, N_PAIRS, s // (nsub * tq)),
        in_specs=[
            pl.BlockSpec((1, nsub * tq, PAIR), lambda bi, p, i: (bi, i, col0 + p)),
            pl.BlockSpec((1, s, PAIR), lambda bi, p, i: (bi, 0, col0 + N_PAIRS + p)),
            pl.BlockSpec((1, 2, VT_ROWS, s), lambda bi, p, i: (bi, p, 0, 0)),
            pl.BlockSpec((1, 2, nblk * tq, tq), lambda bi, p, i: (jnp.minimum(nsub * i, 2), p, 0, 0)),
            pl.BlockSpec((1, 2, nblk * tq, tq), lambda bi, p, i: (jnp.minimum(nsub * i + 1, 2), p, 0, 0)),
            pl.BlockSpec((1, 2, nblk * tq, tq), lambda bi, p, i: (2, p, 0, 0)),
        ],
        out_specs=pl.BlockSpec((1, nsub * tq, PAIR), lambda bi, p, i: (bi, i, p)),
        out_shape=jax.ShapeDtypeStruct((b, s, WIDTH), BF16),
        compiler_params=_params("parallel", "parallel", "arbitrary"),
        name="chunk_attn",
    )(proj3, proj3, vtc, bias_tables, bias_tables, bias_tables)


def _mix_kernel(oa_ref, oc_ref, ga_ref, gc_ref, x_ref, mod_ref, wa_ref, wc_ref, wo_ref,
                g_ref, b_ref, o_ref, *, alpha):
    ya = _dot(oa_ref[...], wa_ref[...])
    yc = _dot(oc_ref[...], wc_ref[...])
    merged = (jax.nn.sigmoid(ga_ref[...].astype(F32)) * ya
              + jax.nn.sigmoid(gc_ref[...].astype(F32)) * yc)
    mix = _dot(merged.astype(BF16), wo_ref[...])
    z = alpha * x_ref[...] + (1.0 + mod_ref[0, 2:3, :]) * mix
    o_ref[...] = _ln(z) * g_ref[...] + b_ref[...]


def _mix(oa, oc, proj, xr, mod, w_a, w_c, w_o, gain, bias, seq, alpha):
    rows, d = xr.shape
    tm = TM_MIX
    tiles_per_seq = seq // tm
    gate_blk = (2 * 3 * WIDTH) // d
    const = lambda i: (0, 0)
    return pl.pallas_call(
        functools.partial(_mix_kernel, alpha=alpha),
        grid=(rows // tm,),
        in_specs=[
            pl.BlockSpec((tm, WIDTH), lambda i: (i, 0)),
            pl.BlockSpec((tm, WIDTH), lambda i: (i, 0)),
            pl.BlockSpec((tm, d), lambda i: (i, gate_blk)),
            pl.BlockSpec((tm, d), lambda i: (i, gate_blk + 1)),
            pl.BlockSpec((tm, d), lambda i: (i, 0)),
            pl.BlockSpec((1, N_MOD, d), lambda i: (i // tiles_per_seq, 0, 0)),
            pl.BlockSpec((WIDTH, d), const),
            pl.BlockSpec((WIDTH, d), const),
            pl.BlockSpec((d, d), const),
            pl.BlockSpec((1, d), const),
            pl.BlockSpec((1, d), const),
        ],
        out_specs=pl.BlockSpec((tm, d), lambda i: (i, 0)),
        out_shape=jax.ShapeDtypeStruct((rows, d), F32),
        compiler_params=_params("parallel"),
        name="mix_out",
    )(oa, oc, proj, proj, xr, mod, w_a, w_c, w_o, gain, bias)


def _ffn_kernel(x_ref, xh_ref, mod_ref, wu_ref, cw_ref, cb_ref, wd_ref, g_ref, b_ref, o_ref, h_ref, acc_ref,
                *, alpha, tm, ck):
    t = pl.program_id(1)
    d_ff = wd_ref.shape[0]
    x = x_ref[0]
    sc = 1.0 + mod_ref[0, 4:5, :]
    sh = mod_ref[0, 3:4, :]
    h_ref[HALO:, :] = (_ln(x) * sc + sh).astype(BF16)
    keep = jnp.where(t > 0, 1.0, 0.0)
    h_ref[0:HALO, :] = ((_ln(xh_ref[0]) * sc + sh) * keep).astype(BF16)
    acc_ref[...] = jnp.zeros_like(acc_ref)

    def conv(u, lo_col):
        cols = slice(lo_col, lo_col + ck)
        lo = HALO - (CONV_WIDTH - 1)
        y = cb_ref[:, cols]
        for j in range(CONV_WIDTH):
            y = y + cw_ref[j:j + 1, cols] * u[lo + j:lo + j + tm, :]
        return y

    def up(c):
        he = h_ref[...]
        return (_dot(he, wu_ref[:, c * ck:(c + 1) * ck]),
                _dot(he, wu_ref[:, d_ff + c * ck:d_ff + (c + 1) * ck]))

    u = up(0)
    for c in range(d_ff // ck):
        u_next = up(c + 1) if (c + 1) * ck < d_ff else None
        a = conv(u[0], c * ck)
        v = conv(u[1], d_ff + c * ck)
        act = (a * jax.nn.sigmoid(a) * v).astype(BF16)
        acc_ref[...] += _dot(act, wd_ref[c * ck:(c + 1) * ck, :])
        u = u_next
    z = alpha * x + (1.0 + mod_ref[0, 5:6, :]) * acc_ref[...]
    o_ref[0] = _ln(z) * g_ref[...] + b_ref[...]


def _ffn(x3, mod, w_up, conv_w, conv_b, w_down, gain, bias, alpha):
    b, s, d = x3.shape
    tm = TM_FFN
    d_ff = w_down.shape[0]
    assert d_ff % CK_FFN == 0
    halo_per_tile = tm // HALO
    c2 = lambda bi, t: (0, 0)
    return pl.pallas_call(
        functools.partial(_ffn_kernel, alpha=alpha, tm=tm, ck=CK_FFN),
        grid=(b, s // tm),
        in_specs=[
            pl.BlockSpec((1, tm, d), lambda bi, t: (bi, t, 0)),
            pl.BlockSpec((1, HALO, d), lambda bi, t: (bi, jnp.maximum(t * halo_per_tile - 1, 0), 0)),
            pl.BlockSpec((1, N_MOD, d), lambda bi, t: (bi, 0, 0)),
            pl.BlockSpec((d, 2 * d_ff), c2),
            pl.BlockSpec((CONV_WIDTH, 2 * d_ff), c2),
            pl.BlockSpec((1, 2 * d_ff), c2),
            pl.BlockSpec((d_ff, d), c2),
            pl.BlockSpec((1, d), c2),
            pl.BlockSpec((1, d), c2),
        ],
        out_specs=pl.BlockSpec((1, tm, d), lambda bi, t: (bi, t, 0)),
        out_shape=jax.ShapeDtypeStruct((b, s, d), F32),
        scratch_shapes=[pltpu.VMEM((tm + HALO, d), BF16), pltpu.VMEM((tm, d), F32)],
        compiler_params=_params("parallel", "parallel"),
        name="conv_ffn",
    )(x3, x3, mod, w_up, conv_w, conv_b, w_down, gain, bias)


def kernel(x, c, w_in, b_f, rel_bias, w_br_fox, w_br_chunk, w_out, w_up, conv_w, conv_b, w_down,
           w_ada, b_ada, ln1_g, ln1_b, ln2_g, ln2_b):
    b, s, d = x.shape
    depth = w_in.shape[0]
    alpha = (2.0 * depth) ** 0.25
    rows = b * s
    mod_all = _ada_mod(c, w_ada, b_ada).reshape(depth, b, N_MOD, d)

    f0 = 3 * WIDTH
    f1 = f0 + N_HEADS
    xr = x.reshape(rows, d)
    for l in range(depth):
        mod = mod_all[l]
        q_scale = LOG2E / math.sqrt(HEAD_DIM)
        qc0 = f1
        qc1 = f1 + WIDTH
        w_main = jnp.concatenate([w_in[l, :, :WIDTH] * q_scale, w_in[l, :, WIDTH:f0],
                                  w_in[l, :, qc0:qc1] * q_scale, w_in[l, :, qc1:]], axis=1).astype(BF16)
        w_f = jnp.pad(w_in[l, :, f0:f1], ((0, 0), (0, LANES - N_HEADS))).astype(BF16)
        bf_pad = jnp.pad(b_f[l], (0, LANES - N_HEADS)).reshape(1, LANES)

        proj, f = _ln_proj(xr, mod, w_main, w_f, s)
        proj3 = proj.reshape(b, s, MAIN_COLS)
        cqt, fs, vt, vtc = _attn_prep(f.reshape(b, s, LANES), bf_pad, proj3)
        o_a = _fox(proj3, fs, vt, cqt)
        o_c = _chunk_attn(proj3, vtc, _chunk_bias_tables(rel_bias[l], TQ_CHUNK))
        xr = _mix(o_a.reshape(rows, WIDTH), o_c.reshape(rows, WIDTH), proj, xr, mod,
                  w_br_fox[l].astype(BF16), w_br_chunk[l].astype(BF16), w_out[l].astype(BF16),
                  ln1_g[l].reshape(1, d), ln1_b[l].reshape(1, d), s, alpha)

        x3 = _ffn(xr.reshape(b, s, d), mod, w_up[l].astype(BF16), conv_w[l], conv_b[l].reshape(1, -1),
                  w_down[l].astype(BF16), ln2_g[l].reshape(1, d), ln2_b[l].reshape(1, d), alpha)
        xr = x3.reshape(rows, d)
    return xr.reshape(b, s, d)
```

```python
import functools
import math

import numpy as np
import jax
import jax.numpy as jnp
from jax import lax
from jax.experimental import pallas as pl
from jax.experimental.pallas import tpu as pltpu

F32 = jnp.float32
BF16 = jnp.bfloat16

D_MODEL = 1024
HEAD_DIM = 64
N_HEADS = 8
WIDTH = N_HEADS * HEAD_DIM
CHUNK = 64
LEFT_CHUNKS = 8
REL_CLIP = 128
CONV_WIDTH = 3
LN_EPS = 1e-5
N_MOD = 6
LANES = 128
PAIR = 2 * HEAD_DIM
N_PAIRS = N_HEADS // 2
MAIN_COLS = 3 * WIDTH + 3 * WIDTH + 2 * D_MODEL
LOG2E = math.log2(math.e)
NEG = -1e30
VMEM_LIMIT = 56 * 1024 * 1024

TM_PROJ = 1024
TN_PROJ = 1024
T_CUM = 512
TQ_FOX = 512
TK_FOX = 512
TQ_CHUNK = 256
NSUB_CHUNK = 8
TM_MIX = 512
TM_FFN = 512
CK_FFN = 256
VT_ROWS = HEAD_DIM + 16
HALO = 16


def _ln(x):
    mu = jnp.mean(x, axis=-1, keepdims=True)
    xc = x - mu
    var = jnp.mean(xc * xc, axis=-1, keepdims=True)
    return xc * lax.rsqrt(var + LN_EPS)


def _dot(a, b):
    return jnp.dot(a, b, preferred_element_type=F32)


def _split3(v):
    hi = v.astype(BF16)
    r = v - hi.astype(F32)
    mid = r.astype(BF16)
    lo = (r - mid.astype(F32)).astype(BF16)
    return hi, mid, lo


def _params(*sem):
    return pltpu.CompilerParams(dimension_semantics=sem, vmem_limit_bytes=VMEM_LIMIT)


def _mod_kernel(c_ref, w_ref, b_ref, o_ref):
    c = c_ref[...]
    cond = c * jax.nn.sigmoid(c)
    a_hi, a_mid, _ = _split3(cond)
    w = w_ref[0]
    w_hi = w.astype(BF16)
    w_lo = (w - w_hi.astype(F32)).astype(BF16)
    acc = _dot(a_hi, w_hi) + _dot(a_mid, w_hi) + _dot(a_hi, w_lo)
    o_ref[0] = acc + b_ref[0]


def _ada_mod(c, w_ada, b_ada):
    depth, d, n = w_ada.shape
    b = c.shape[0]
    rows = 8
    c_pad = jnp.pad(c, ((0, rows - b), (0, 0)))
    tn = 1024
    out = pl.pallas_call(
        _mod_kernel,
        grid=(depth, n // tn),
        in_specs=[
            pl.BlockSpec((rows, d), lambda l, j: (0, 0)),
            pl.BlockSpec((1, d, tn), lambda l, j: (l, 0, j)),
            pl.BlockSpec((1, 1, tn), lambda l, j: (l, 0, j)),
        ],
        out_specs=pl.BlockSpec((1, rows, tn), lambda l, j: (l, 0, j)),
        out_shape=jax.ShapeDtypeStruct((depth, rows, n), F32),
        compiler_params=_params("parallel", "parallel"),
        name="ada_mod",
    )(c_pad, w_ada, b_ada.reshape(depth, 1, n))
    return out[:, :b, :]


def _ln_proj_kernel(x_ref, mod_ref, w_ref, wf_ref, o_ref, f_ref, h_ref):
    @pl.when(pl.program_id(1) == 0)
    def _():
        y = _ln(x_ref[...])
        h = (y * (1.0 + mod_ref[0, 1:2, :]) + mod_ref[0, 0:1, :]).astype(BF16)
        h_ref[...] = h
        f_ref[...] = _dot(h, wf_ref[...])

    o_ref[...] = _dot(h_ref[...], w_ref[...]).astype(BF16)


def _ln_proj(xr, mod, w_main, w_f, seq):
    rows, d = xr.shape
    tm, tn = TM_PROJ, TN_PROJ
    n = w_main.shape[1]
    tiles_per_seq = seq // tm
    return pl.pallas_call(
        _ln_proj_kernel,
        grid=(rows // tm, n // tn),
        in_specs=[
            pl.BlockSpec((tm, d), lambda i, j: (i, 0)),
            pl.BlockSpec((1, N_MOD, d), lambda i, j: (i // tiles_per_seq, 0, 0)),
            pl.BlockSpec((d, tn), lambda i, j: (0, j)),
            pl.BlockSpec((d, LANES), lambda i, j: (0, 0)),
        ],
        out_specs=[
            pl.BlockSpec((tm, tn), lambda i, j: (i, j)),
            pl.BlockSpec((tm, LANES), lambda i, j: (i, 0)),
        ],
        out_shape=[
            jax.ShapeDtypeStruct((rows, n), BF16),
            jax.ShapeDtypeStruct((rows, LANES), F32),
        ],
        scratch_shapes=[pltpu.VMEM((tm, d), BF16)],
        compiler_params=_params("parallel", "arbitrary"),
        name="ln_proj",
    )(xr, mod, w_main, w_f)


def _attn_prep_kernel(f_ref, bf_ref, v_ref, vc_ref, cqt_ref, fs_ref, vt_ref, vtc_ref, carry_ref):
    @pl.when(pl.program_id(1) == 0)
    def _():
        carry_ref[...] = jnp.zeros_like(carry_ref)

    z = f_ref[0] + bf_ref[...]
    lf = jnp.minimum(z, 0.0) - jnp.log(1.0 + jnp.exp(-jnp.abs(z)))
    hi, mid, lo = _split3(lf)
    tc = lf.shape[0]
    row = lax.broadcasted_iota(jnp.int32, (tc, tc), 0)
    col = lax.broadcasted_iota(jnp.int32, (tc, tc), 1)
    tri = jnp.where(col <= row, 1.0, 0.0).astype(BF16)
    cum = _dot(tri, hi) + _dot(tri, mid) + _dot(tri, lo) + carry_ref[...]
    carry_ref[...] = cum[tc - 1:tc, :]

    cum2 = cum * LOG2E
    cqt_ref[0] = cum2.T[0:N_HEADS, :]
    neg = -cum2
    n_hi = neg.astype(BF16).astype(F32)
    r1 = neg - n_hi
    n_mid = r1.astype(BF16).astype(F32)
    n_lo = r1 - n_mid
    lane = lax.broadcasted_iota(jnp.int32, (1, LANES), 1)
    placed = jnp.where(lane < N_HEADS, n_hi,
                       jnp.where(lane < 2 * N_HEADS, pltpu.roll(n_mid, N_HEADS, axis=1),
                                 jnp.where(lane < 3 * N_HEADS, pltpu.roll(n_lo, 2 * N_HEADS, axis=1), 0.0)))
    fs_ref[0] = placed.astype(BF16)

    extra = VT_ROWS - HEAD_DIM
    ones_row = jnp.where(lax.broadcasted_iota(jnp.int32, (extra, tc), 0) == 0, 1.0, 0.0)
    for src_ref, dst_ref in ((v_ref, vt_ref), (vc_ref, vtc_ref)):
        for p in range(N_PAIRS):
            vt = src_ref[0, :, p * PAIR:(p + 1) * PAIR].astype(F32).T
            dst_ref[0, 2 * p] = jnp.concatenate([vt[0:HEAD_DIM], ones_row], axis=0).astype(BF16)
            dst_ref[0, 2 * p + 1] = jnp.concatenate([vt[HEAD_DIM:], ones_row], axis=0).astype(BF16)


def _attn_prep(f3, bf_pad, proj3):
    b, s, _ = f3.shape
    tc = T_CUM
    v_blk = 2
    vc_blk = 5
    return pl.pallas_call(
        _attn_prep_kernel,
        grid=(b, s // tc),
        in_specs=[
            pl.BlockSpec((1, tc, LANES), lambda bi, t: (bi, t, 0)),
            pl.BlockSpec((1, LANES), lambda bi, t: (0, 0)),
            pl.BlockSpec((1, tc, WIDTH), lambda bi, t: (bi, t, v_blk)),
            pl.BlockSpec((1, tc, WIDTH), lambda bi, t: (bi, t, vc_blk)),
        ],
        out_specs=[
            pl.BlockSpec((1, N_HEADS, tc), lambda bi, t: (bi, 0, t)),
            pl.BlockSpec((1, tc, LANES), lambda bi, t: (bi, t, 0)),
            pl.BlockSpec((1, N_HEADS, VT_ROWS, tc), lambda bi, t: (bi, 0, 0, t)),
            pl.BlockSpec((1, N_HEADS, VT_ROWS, tc), lambda bi, t: (bi, 0, 0, t)),
        ],
        out_shape=[
            jax.ShapeDtypeStruct((b, N_HEADS, s), F32),
            jax.ShapeDtypeStruct((b, s, LANES), BF16),
            jax.ShapeDtypeStruct((b, N_HEADS, VT_ROWS, s), BF16),
            jax.ShapeDtypeStruct((b, N_HEADS, VT_ROWS, s), BF16),
        ],
        scratch_shapes=[pltpu.VMEM((1, LANES), F32)],
        compiler_params=_params("parallel", "arbitrary"),
        name="attn_prep",
    )(f3, bf_pad, proj3, proj3)


def _fox_kernel(q_ref, k_ref, fs_ref, vt_ref, cqt_ref, o_ref, qt_ref, sta_ref, stb_ref, cma_ref, cmb_ref,
                m_ref, acc_ref, *, tq, tk):
    p = pl.program_id(1)
    i = pl.program_id(2)
    q0 = i * tq
    n_full = q0 // tk
    qt = q_ref[0].astype(F32).T
    row = lax.broadcasted_iota(jnp.int32, (PAIR, 1), 0)
    qts, cqs = [], []
    for hh in range(2):
        h = 2 * p + hh
        in_head = (row < HEAD_DIM) if hh == 0 else (row >= HEAD_DIM)
        q_rows = jnp.where(in_head, qt, 0.0)
        pick = (row == h) | (row == h + N_HEADS) | (row == h + 2 * N_HEADS)
        one_rows = jnp.broadcast_to(jnp.where(pick, 1.0, 0.0), (PAIR, tq))
        qts.append(jnp.concatenate([q_rows, one_rows], axis=0).astype(BF16))
        cqs.append(cqt_ref[0, pl.ds(h, 1), :])
    qt_both = jnp.concatenate(qts, axis=1)
    cq = jnp.concatenate(cqs, axis=1)

    qt_ref[...] = qt_both
    m_ref[...] = jnp.full((1, 2 * tq), NEG, F32)
    acc_ref[...] = jnp.zeros((2, VT_ROWS, tq), F32)

    def qk(j, st_ref, cm_ref):
        ks = pl.multiple_of(j * tk, tk)
        kf = jnp.concatenate([k_ref[0, pl.ds(ks, tk), :], fs_ref[0, pl.ds(ks, tk), :]], axis=1)
        st = _dot(kf, qt_ref[...])
        st_ref[...] = st
        cm_ref[...] = jnp.max(st, axis=0, keepdims=True)

    def soft_pv(j, st_ref, cm_ref):
        ks = pl.multiple_of(j * tk, tk)
        m = m_ref[...]
        m_new = jnp.maximum(m, cm_ref[...] + cq)
        alpha = jnp.exp2(m - m_new)
        pb = jnp.exp2(st_ref[...] - (m_new - cq)).astype(BF16)
        m_ref[...] = m_new
        acc_ref[0] = alpha[:, :tq] * acc_ref[0] + _dot(vt_ref[0, 0, :, pl.ds(ks, tk)], pb[:, :tq])
        acc_ref[1] = alpha[:, tq:] * acc_ref[1] + _dot(vt_ref[0, 1, :, pl.ds(ks, tk)], pb[:, tq:])

    half = tq // 2
    ks_diag = pl.multiple_of(n_full * tk, tk)

    def qk_diag(st_ref):
        kf = jnp.concatenate([k_ref[0, pl.ds(ks_diag, tk), :], fs_ref[0, pl.ds(ks_diag, tk), :]], axis=1)
        qt_all = qt_ref[...]
        st_ref[0:half, :] = _dot(kf[0:half], qt_all)
        qt_hi = jnp.concatenate([qt_all[:, half:tq], qt_all[:, tq + half:]], axis=1)
        hi = _dot(kf[half:], qt_hi)
        st_ref[half:, half:tq] = hi[:, 0:half]
        st_ref[half:, tq + half:] = hi[:, half:]

    def soft_pv_diag(st_ref):
        causal = (lax.broadcasted_iota(jnp.int32, (half, half), 0)
                  <= lax.broadcasted_iota(jnp.int32, (half, half), 1))
        for hh in range(2):
            for c in range(2):
                cols = slice(hh * tq + c * half, hh * tq + (c + 1) * half)
                diag = jnp.where(causal, st_ref[c * half:(c + 1) * half, cols], NEG)
                parts = [diag] if c == 0 else [st_ref[0:half, cols], diag]
                cm = functools.reduce(jnp.maximum, [jnp.max(part, axis=0, keepdims=True) for part in parts])
                m = m_ref[:, cols]
                cq_c = cq[:, cols]
                m_new = jnp.maximum(m, cm + cq_c)
                alpha = jnp.exp2(m - m_new)
                shift = m_new - cq_c
                pb = jnp.concatenate([jnp.exp2(part - shift) for part in parts], axis=0).astype(BF16)
                m_ref[:, cols] = m_new
                out_cols = slice(c * half, (c + 1) * half)
                acc_ref[hh, :, out_cols] = (alpha * acc_ref[hh, :, out_cols]
                                            + _dot(vt_ref[0, hh, :, pl.ds(ks_diag, (c + 1) * half)], pb))

    qk(0, sta_ref, cma_ref)

    def pair(jj, c):
        qk(2 * jj + 1, stb_ref, cmb_ref)
        soft_pv(2 * jj, sta_ref, cma_ref)
        qk(2 * jj + 2, sta_ref, cma_ref)
        soft_pv(2 * jj + 1, stb_ref, cmb_ref)
        return c

    lax.fori_loop(0, n_full // 2, pair, 0)

    @pl.when(n_full % 2 == 1)
    def _():
        qk_diag(stb_ref)
        soft_pv(n_full - 1, sta_ref, cma_ref)
        soft_pv_diag(stb_ref)

    @pl.when(n_full % 2 == 0)
    def _():
        soft_pv_diag(sta_ref)

    ot = jnp.concatenate([acc_ref[hh, 0:HEAD_DIM, :] * (1.0 / acc_ref[hh, HEAD_DIM:HEAD_DIM + 1, :])
                          for hh in range(2)], axis=0)
    o_ref[0] = ot.T.astype(BF16)


def _fox(proj3, fs, vt, cqt):
    b, s, _ = proj3.shape
    tq, tk = TQ_FOX, TK_FOX
    assert tk == tq and s % tk == 0
    return pl.pallas_call(
        functools.partial(_fox_kernel, tq=tq, tk=tk),
        grid=(b, N_PAIRS, s // tq),
        in_specs=[
            pl.BlockSpec((1, tq, PAIR), lambda bi, p, i: (bi, i, p)),
            pl.BlockSpec((1, s, PAIR), lambda bi, p, i: (bi, 0, N_PAIRS + p)),
            pl.BlockSpec((1, s, LANES), lambda bi, p, i: (bi, 0, 0)),
            pl.BlockSpec((1, 2, VT_ROWS, s), lambda bi, p, i: (bi, p, 0, 0)),
            pl.BlockSpec((1, N_HEADS, tq), lambda bi, p, i: (bi, 0, i)),
        ],
        out_specs=pl.BlockSpec((1, tq, PAIR), lambda bi, p, i: (bi, i, p)),
        out_shape=jax.ShapeDtypeStruct((b, s, WIDTH), BF16),
        scratch_shapes=[pltpu.VMEM((2 * PAIR, 2 * tq), BF16), pltpu.VMEM((tk, 2 * tq), F32),
                        pltpu.VMEM((tk, 2 * tq), F32), pltpu.VMEM((1, 2 * tq), F32),
                        pltpu.VMEM((1, 2 * tq), F32), pltpu.VMEM((1, 2 * tq), F32),
                        pltpu.VMEM((2, VT_ROWS, tq), F32)],
        compiler_params=_params("parallel", "parallel", "arbitrary"),
        name="fox_attn",
    )(proj3, proj3, fs, vt, cqt)


def _chunk_kernel(q_ref, k_ref, vt_ref, bias_a_ref, bias_b_ref, bias_ref, o_ref, *, tq, nblk, nsub):
    i = pl.program_id(2)
    w = nblk * tq
    qt = q_ref[0].astype(F32).T
    row = lax.broadcasted_iota(jnp.int32, (PAIR, 1), 0)
    head_rows = [row < HEAD_DIM, row >= HEAD_DIM]
    bias_refs = [bias_a_ref, bias_b_ref] + [bias_ref] * (nsub - 2)

    def scores(sb):
        g = nsub * i + sb
        ws = pl.multiple_of(jnp.maximum(g - (nblk - 1), 0) * tq, tq)
        qs = qt[:, sb * tq:(sb + 1) * tq]
        qt_both = jnp.concatenate([jnp.where(hr, qs, 0.0) for hr in head_rows], axis=1).astype(BF16)
        st = _dot(k_ref[0, pl.ds(ws, w), :], qt_both)
        return ws, [st[:, hh * tq:(hh + 1) * tq] + bias_refs[sb][0, 0, hh] for hh in range(2)]

    def finish(sb, ws, sts):
        cols = []
        for hh in range(2):
            st = sts[hh]
            pb = jnp.exp2(st - jnp.max(st, axis=0, keepdims=True)).astype(BF16)
            acc = _dot(vt_ref[0, hh, :, pl.ds(ws, w)], pb)
            cols.append(acc[0:HEAD_DIM] * (1.0 / acc[HEAD_DIM:HEAD_DIM + 1]))
        o_ref[0, sb * tq:(sb + 1) * tq, :] = jnp.concatenate(cols, axis=0).T.astype(BF16)

    pending = scores(0)
    for sb in range(nsub):
        nxt = scores(sb + 1) if sb + 1 < nsub else None
        finish(sb, *pending)
        pending = nxt


def _chunk_bias_tables(rel_bias, tq):
    left = LEFT_CHUNKS * CHUNK
    w = tq + left
    row_len = 1 << (tq + w - 2).bit_length()
    period = row_len + 1
    lead = rel_bias.shape[:-1]
    top = rel_bias[..., 2 * REL_CLIP:]
    bot = rel_bias[..., :1]
    n_top = left - REL_CLIP + 1
    n_bot = w - n_top - 2 * REL_CLIP
    u = jnp.concatenate([
        jnp.broadcast_to(top, lead + (n_top,)),
        rel_bias[..., :2 * REL_CLIP][..., ::-1],
        jnp.broadcast_to(bot, lead + (n_bot + period - w - (tq - 1),)),
        jnp.broadcast_to(top, lead + (tq - 1,)),
    ], axis=-1).astype(F32)
    flat = jnp.tile(u, (1,) * len(lead) + (tq + 1,))[..., :tq * row_len]
    bias = flat.reshape(lead + (tq, row_len))[..., :w]
    qc = np.arange(tq)[:, None] // CHUNK
    kc = np.arange(w)[None, :] // CHUNK
    band = (kc >= qc) & (kc <= qc + LEFT_CHUNKS)
    full = jnp.swapaxes(jnp.where(jnp.asarray(band), bias * LOG2E, NEG), -1, -2)
    gone = jnp.full(lead + (tq, tq), NEG, F32)
    v1 = jnp.concatenate([full[..., tq:, :], gone], axis=-2)
    v0 = jnp.concatenate([full[..., 2 * tq:, :], gone, gone], axis=-2)
    return jnp.stack([v0, v1, full], axis=1)


def _chunk_attn(proj3, vtc, bias_tables, layer):
    b, s, _ = proj3.shape
    tq = TQ_CHUNK
    left = LEFT_CHUNKS * CHUNK
    assert left == 2 * tq and s >= 3 * tq
    nblk = 3
    nsub = NSUB_CHUNK
    assert nsub >= 2 and s % (nsub * tq) == 0
    col0 = 3 * N_PAIRS
    return pl.pallas_call(
        functools.partial(_chunk_kernel, tq=tq, nblk=nblk, nsub=nsub),
        grid=(b, N_PAIRS, s // (nsub * tq)),
        in_specs=[
            pl.BlockSpec((1, nsub * tq, PAIR), lambda bi, p, i: (bi, i, col0 + p)),
            pl.BlockSpec((1, s, PAIR), lambda bi, p, i: (bi, 0, col0 + N_PAIRS + p)),
            pl.BlockSpec((1, 2, VT_ROWS, s), lambda bi, p, i: (bi, p, 0, 0)),
            pl.BlockSpec((1, 1, 2, nblk * tq, tq), lambda bi, p, i: (layer, jnp.minimum(nsub * i, 2), p, 0, 0)),
            pl.BlockSpec((1, 1, 2, nblk * tq, tq), lambda bi, p, i: (layer, jnp.minimum(nsub * i + 1, 2), p, 0, 0)),
            pl.BlockSpec((1, 1, 2, nblk * tq, tq), lambda bi, p, i: (layer, 2, p, 0, 0)),
        ],
        out_specs=pl.BlockSpec((1, nsub * tq, PAIR), lambda bi, p, i: (bi, i, p)),
        out_shape=jax.ShapeDtypeStruct((b, s, WIDTH), BF16),
        compiler_params=_params("parallel", "parallel", "arbitrary"),
        name="chunk_attn",
    )(proj3, proj3, vtc, bias_tables, bias_tables, bias_tables)


def _mix_kernel(oa_ref, oc_ref, ga_ref, gc_ref, x_ref, mod_ref, wa_ref, wc_ref, wo_ref,
                g_ref, b_ref, o_ref, *, alpha):
    ya = _dot(oa_ref[...], wa_ref[...])
    yc = _dot(oc_ref[...], wc_ref[...])
    merged = (jax.nn.sigmoid(ga_ref[...].astype(F32)) * ya
              + jax.nn.sigmoid(gc_ref[...].astype(F32)) * yc)
    mix = _dot(merged.astype(BF16), wo_ref[...])
    z = alpha * x_ref[...] + (1.0 + mod_ref[0, 2:3, :]) * mix
    o_ref[...] = _ln(z) * g_ref[...] + b_ref[...]


def _mix(oa, oc, proj, xr, mod, w_a, w_c, w_o, gain, bias, seq, alpha):
    rows, d = xr.shape
    tm = TM_MIX
    tiles_per_seq = seq // tm
    gate_blk = (2 * 3 * WIDTH) // d
    const = lambda i: (0, 0)
    return pl.pallas_call(
        functools.partial(_mix_kernel, alpha=alpha),
        grid=(rows // tm,),
        in_specs=[
            pl.BlockSpec((tm, WIDTH), lambda i: (i, 0)),
            pl.BlockSpec((tm, WIDTH), lambda i: (i, 0)),
            pl.BlockSpec((tm, d), lambda i: (i, gate_blk)),
            pl.BlockSpec((tm, d), lambda i: (i, gate_blk + 1)),
            pl.BlockSpec((tm, d), lambda i: (i, 0)),
            pl.BlockSpec((1, N_MOD, d), lambda i: (i // tiles_per_seq, 0, 0)),
            pl.BlockSpec((WIDTH, d), const),
            pl.BlockSpec((WIDTH, d), const),
            pl.BlockSpec((d, d), const),
            pl.BlockSpec((1, d), const),
            pl.BlockSpec((1, d), const),
        ],
        out_specs=pl.BlockSpec((tm, d), lambda i: (i, 0)),
        out_shape=jax.ShapeDtypeStruct((rows, d), F32),
        compiler_params=_params("parallel"),
        name="mix_out",
    )(oa, oc, proj, proj, xr, mod, w_a, w_c, w_o, gain, bias)


def _ffn_kernel(x_ref, xh_ref, mod_ref, wu_ref, cw_ref, cb_ref, wd_ref, g_ref, b_ref, o_ref, h_ref, acc_ref,
                *, alpha, tm, ck):
    t = pl.program_id(1)
    d_ff = wd_ref.shape[0]
    x = x_ref[0]
    sc = 1.0 + mod_ref[0, 4:5, :]
    sh = mod_ref[0, 3:4, :]
    h_ref[HALO:, :] = (_ln(x) * sc + sh).astype(BF16)
    keep = jnp.where(t > 0, 1.0, 0.0)
    h_ref[0:HALO, :] = ((_ln(xh_ref[0]) * sc + sh) * keep).astype(BF16)
    acc_ref[...] = jnp.zeros_like(acc_ref)

    def conv(u, lo_col):
        cols = slice(lo_col, lo_col + ck)
        lo = HALO - (CONV_WIDTH - 1)
        y = cb_ref[:, cols]
        for j in range(CONV_WIDTH):
            y = y + cw_ref[j:j + 1, cols] * u[lo + j:lo + j + tm, :]
        return y

    def up(c):
        he = h_ref[...]
        return (_dot(he, wu_ref[:, c * ck:(c + 1) * ck]),
                _dot(he, wu_ref[:, d_ff + c * ck:d_ff + (c + 1) * ck]))

    u = up(0)
    for c in range(d_ff // ck):
        u_next = up(c + 1) if (c + 1) * ck < d_ff else None
        a = conv(u[0], c * ck)
        v = conv(u[1], d_ff + c * ck)
        act = (a * jax.nn.sigmoid(a) * v).astype(BF16)
        acc_ref[...] += _dot(act, wd_ref[c * ck:(c + 1) * ck, :])
        u = u_next
    z = alpha * x + (1.0 + mod_ref[0, 5:6, :]) * acc_ref[...]
    o_ref[0] = _ln(z) * g_ref[...] + b_ref[...]


def _ffn(x3, mod, w_up, conv_w, conv_b, w_down, gain, bias, alpha):
    b, s, d = x3.shape
    tm = TM_FFN
    d_ff = w_down.shape[0]
    assert d_ff % CK_FFN == 0
    halo_per_tile = tm // HALO
    c2 = lambda bi, t: (0, 0)
    return pl.pallas_call(
        functools.partial(_ffn_kernel, alpha=alpha, tm=tm, ck=CK_FFN),
        grid=(b, s // tm),
        in_specs=[
            pl.BlockSpec((1, tm, d), lambda bi, t: (bi, t, 0)),
            pl.BlockSpec((1, HALO, d), lambda bi, t: (bi, jnp.maximum(t * halo_per_tile - 1, 0), 0)),
            pl.BlockSpec((1, N_MOD, d), lambda bi, t: (bi, 0, 0)),
            pl.BlockSpec((d, 2 * d_ff), c2),
            pl.BlockSpec((CONV_WIDTH, 2 * d_ff), c2),
            pl.BlockSpec((1, 2 * d_ff), c2),
            pl.BlockSpec((d_ff, d), c2),
            pl.BlockSpec((1, d), c2),
            pl.BlockSpec((1, d), c2),
        ],
        out_specs=pl.BlockSpec((1, tm, d), lambda bi, t: (bi, t, 0)),
        out_shape=jax.ShapeDtypeStruct((b, s, d), F32),
        scratch_shapes=[pltpu.VMEM((tm + HALO, d), BF16), pltpu.VMEM((tm, d), F32)],
        compiler_params=_params("parallel", "parallel"),
        name="conv_ffn",
    )(x3, x3, mod, w_up, conv_w, conv_b, w_down, gain, bias)


def kernel(x, c, w_in, b_f, rel_bias, w_br_fox, w_br_chunk, w_out, w_up, conv_w, conv_b, w_down,
           w_ada, b_ada, ln1_g, ln1_b, ln2_g, ln2_b):
    b, s, d = x.shape
    depth = w_in.shape[0]
    alpha = (2.0 * depth) ** 0.25
    rows = b * s
    mod_all = _ada_mod(c, w_ada, b_ada).reshape(depth, b, N_MOD, d)
    bias_tables = _chunk_bias_tables(rel_bias, TQ_CHUNK)

    f0 = 3 * WIDTH
    f1 = f0 + N_HEADS
    xr = x.reshape(rows, d)
    for l in range(depth):
        mod = mod_all[l]
        q_scale = LOG2E / math.sqrt(HEAD_DIM)
        qc0 = f1
        qc1 = f1 + WIDTH
        w_main = jnp.concatenate([w_in[l, :, :WIDTH] * q_scale, w_in[l, :, WIDTH:f0],
                                  w_in[l, :, qc0:qc1] * q_scale, w_in[l, :, qc1:]], axis=1).astype(BF16)
        w_f = jnp.pad(w_in[l, :, f0:f1], ((0, 0), (0, LANES - N_HEADS))).astype(BF16)
        bf_pad = jnp.pad(b_f[l], (0, LANES - N_HEADS)).reshape(1, LANES)

        proj, f = _ln_proj(xr, mod, w_main, w_f, s)
        proj3 = proj.reshape(b, s, MAIN_COLS)
        cqt, fs, vt, vtc = _attn_prep(f.reshape(b, s, LANES), bf_pad, proj3)
        o_a = _fox(proj3, fs, vt, cqt)
        o_c = _chunk_attn(proj3, vtc, bias_tables, l)
        xr = _mix(o_a.reshape(rows, WIDTH), o_c.reshape(rows, WIDTH), proj, xr, mod,
                  w_br_fox[l].astype(BF16), w_br_chunk[l].astype(BF16), w_out[l].astype(BF16),
                  ln1_g[l].reshape(1, d), ln1_b[l].reshape(1, d), s, alpha)

        x3 = _ffn(xr.reshape(b, s, d), mod, w_up[l].astype(BF16), conv_w[l], conv_b[l].reshape(1, -1),
                  w_down[l].astype(BF16), ln2_g[l].reshape(1, d), ln2_b[l].reshape(1, d), alpha)
        xr = x3.reshape(rows, d)
    return xr.reshape(b, s, d)
```

```python
import functools
import math

import numpy as np
import jax
import jax.numpy as jnp
from jax import lax
from jax.experimental import pallas as pl
from jax.experimental.pallas import tpu as pltpu

F32 = jnp.float32
BF16 = jnp.bfloat16

D_MODEL = 1024
HEAD_DIM = 64
N_HEADS = 8
WIDTH = N_HEADS * HEAD_DIM
CHUNK = 64
LEFT_CHUNKS = 8
REL_CLIP = 128
CONV_WIDTH = 3
LN_EPS = 1e-5
N_MOD = 6
LANES = 128
PAIR = 2 * HEAD_DIM
N_PAIRS = N_HEADS // 2
MAIN_COLS = 3 * WIDTH + 3 * WIDTH + 2 * D_MODEL
LOG2E = math.log2(math.e)
NEG = -1e30
VMEM_LIMIT = 56 * 1024 * 1024

TM_PROJ = 1024
TN_PROJ = 1024
T_CUM = 512
TQ_FOX = 1024
TK_FOX = 1024
TQ_CHUNK = 256
NSUB_CHUNK = 16
TM_MIX = 512
TM_FFN = 512
CK_FFN = 2816
VT_ROWS = HEAD_DIM + 16
HALO = 16


def _ln(x):
    mu = jnp.mean(x, axis=-1, keepdims=True)
    xc = x - mu
    var = jnp.mean(xc * xc, axis=-1, keepdims=True)
    return xc * lax.rsqrt(var + LN_EPS)


def _dot(a, b):
    return jnp.dot(a, b, preferred_element_type=F32)


def _split3(v):
    hi = v.astype(BF16)
    r = v - hi.astype(F32)
    mid = r.astype(BF16)
    lo = (r - mid.astype(F32)).astype(BF16)
    return hi, mid, lo


def _params(*sem):
    return pltpu.CompilerParams(dimension_semantics=sem, vmem_limit_bytes=VMEM_LIMIT)


def _mod_kernel(c_ref, w_ref, b_ref, o_ref):
    c = c_ref[...]
    cond = c * jax.nn.sigmoid(c)
    a_hi, a_mid, _ = _split3(cond)
    w = w_ref[0]
    w_hi = w.astype(BF16)
    w_lo = (w - w_hi.astype(F32)).astype(BF16)
    acc = _dot(a_hi, w_hi) + _dot(a_mid, w_hi) + _dot(a_hi, w_lo)
    o_ref[0] = acc + b_ref[0]


def _ada_mod(c, w_ada, b_ada):
    depth, d, n = w_ada.shape
    b = c.shape[0]
    rows = 8
    c_pad = jnp.pad(c, ((0, rows - b), (0, 0)))
    tn = 1024
    out = pl.pallas_call(
        _mod_kernel,
        grid=(depth, n // tn),
        in_specs=[
            pl.BlockSpec((rows, d), lambda l, j: (0, 0)),
            pl.BlockSpec((1, d, tn), lambda l, j: (l, 0, j)),
            pl.BlockSpec((1, 1, tn), lambda l, j: (l, 0, j)),
        ],
        out_specs=pl.BlockSpec((1, rows, tn), lambda l, j: (l, 0, j)),
        out_shape=jax.ShapeDtypeStruct((depth, rows, n), F32),
        compiler_params=_params("parallel", "parallel"),
        name="ada_mod",
    )(c_pad, w_ada, b_ada.reshape(depth, 1, n))
    return out[:, :b, :]


def _ln_proj_kernel(x_ref, mod_ref, w_ref, wf_ref, o_ref, f_ref, h_ref):
    @pl.when(pl.program_id(1) == 0)
    def _():
        y = _ln(x_ref[...])
        h = (y * (1.0 + mod_ref[0, 1:2, :]) + mod_ref[0, 0:1, :]).astype(BF16)
        h_ref[...] = h
        f_ref[...] = _dot(h, wf_ref[...])

    o_ref[...] = _dot(h_ref[...], w_ref[...]).astype(BF16)


def _ln_proj(xr, mod, w_main, w_f, seq):
    rows, d = xr.shape
    tm, tn = TM_PROJ, TN_PROJ
    n = w_main.shape[1]
    tiles_per_seq = seq // tm
    return pl.pallas_call(
        _ln_proj_kernel,
        grid=(rows // tm, n // tn),
        in_specs=[
            pl.BlockSpec((tm, d), lambda i, j: (i, 0)),
            pl.BlockSpec((1, N_MOD, d), lambda i, j: (i // tiles_per_seq, 0, 0)),
            pl.BlockSpec((d, tn), lambda i, j: (0, j)),
            pl.BlockSpec((d, LANES), lambda i, j: (0, 0)),
        ],
        out_specs=[
            pl.BlockSpec((tm, tn), lambda i, j: (i, j)),
            pl.BlockSpec((tm, LANES), lambda i, j: (i, 0)),
        ],
        out_shape=[
            jax.ShapeDtypeStruct((rows, n), BF16),
            jax.ShapeDtypeStruct((rows, LANES), F32),
        ],
        scratch_shapes=[pltpu.VMEM((tm, d), BF16)],
        compiler_params=_params("parallel", "arbitrary"),
        name="ln_proj",
    )(xr, mod, w_main, w_f)


def _attn_prep_kernel(f_ref, bf_ref, v_ref, vc_ref, cqt_ref, fs_ref, vt_ref, vtc_ref, carry_ref):
    @pl.when(pl.program_id(1) == 0)
    def _():
        carry_ref[...] = jnp.zeros_like(carry_ref)

    z = f_ref[0] + bf_ref[...]
    lf = jnp.minimum(z, 0.0) - jnp.log(1.0 + jnp.exp(-jnp.abs(z)))
    hi, mid, lo = _split3(lf)
    tc = lf.shape[0]
    row = lax.broadcasted_iota(jnp.int32, (tc, tc), 0)
    col = lax.broadcasted_iota(jnp.int32, (tc, tc), 1)
    tri = jnp.where(col <= row, 1.0, 0.0).astype(BF16)
    cum = _dot(tri, hi) + _dot(tri, mid) + _dot(tri, lo) + carry_ref[...]
    carry_ref[...] = cum[tc - 1:tc, :]

    cum2 = cum * LOG2E
    cqt_ref[0] = cum2.T[0:N_HEADS, :]
    neg = -cum2
    n_hi = neg.astype(BF16).astype(F32)
    r1 = neg - n_hi
    n_mid = r1.astype(BF16).astype(F32)
    n_lo = r1 - n_mid
    lane = lax.broadcasted_iota(jnp.int32, (1, LANES), 1)
    placed = jnp.where(lane < N_HEADS, n_hi,
                       jnp.where(lane < 2 * N_HEADS, pltpu.roll(n_mid, N_HEADS, axis=1),
                                 jnp.where(lane < 3 * N_HEADS, pltpu.roll(n_lo, 2 * N_HEADS, axis=1), 0.0)))
    fs_ref[0] = placed.astype(BF16)

    extra = VT_ROWS - HEAD_DIM
    ones_row = jnp.where(lax.broadcasted_iota(jnp.int32, (extra, tc), 0) == 0, 1.0, 0.0)
    for src_ref, dst_ref in ((v_ref, vt_ref), (vc_ref, vtc_ref)):
        for p in range(N_PAIRS):
            vt = src_ref[0, :, p * PAIR:(p + 1) * PAIR].astype(F32).T
            dst_ref[0, 2 * p] = jnp.concatenate([vt[0:HEAD_DIM], ones_row], axis=0).astype(BF16)
            dst_ref[0, 2 * p + 1] = jnp.concatenate([vt[HEAD_DIM:], ones_row], axis=0).astype(BF16)


def _attn_prep(f3, bf_pad, proj3):
    b, s, _ = f3.shape
    tc = T_CUM
    v_blk = 2
    vc_blk = 5
    return pl.pallas_call(
        _attn_prep_kernel,
        grid=(b, s // tc),
        in_specs=[
            pl.BlockSpec((1, tc, LANES), lambda bi, t: (bi, t, 0)),
            pl.BlockSpec((1, LANES), lambda bi, t: (0, 0)),
            pl.BlockSpec((1, tc, WIDTH), lambda bi, t: (bi, t, v_blk)),
            pl.BlockSpec((1, tc, WIDTH), lambda bi, t: (bi, t, vc_blk)),
        ],
        out_specs=[
            pl.BlockSpec((1, N_HEADS, tc), lambda bi, t: (bi, 0, t)),
            pl.BlockSpec((1, tc, LANES), lambda bi, t: (bi, t, 0)),
            pl.BlockSpec((1, N_HEADS, VT_ROWS, tc), lambda bi, t: (bi, 0, 0, t)),
            pl.BlockSpec((1, N_HEADS, VT_ROWS, tc), lambda bi, t: (bi, 0, 0, t)),
        ],
        out_shape=[
            jax.ShapeDtypeStruct((b, N_HEADS, s), F32),
            jax.ShapeDtypeStruct((b, s, LANES), BF16),
            jax.ShapeDtypeStruct((b, N_HEADS, VT_ROWS, s), BF16),
            jax.ShapeDtypeStruct((b, N_HEADS, VT_ROWS, s), BF16),
        ],
        scratch_shapes=[pltpu.VMEM((1, LANES), F32)],
        compiler_params=_params("parallel", "arbitrary"),
        name="attn_prep",
    )(f3, bf_pad, proj3, proj3)


def _fox_kernel(q_ref, k_ref, fs_ref, vt_ref, cqt_ref, o_ref, qt_ref, sta_ref, stb_ref, cma_ref, cmb_ref,
                m_ref, acc_ref, *, tq, tk):
    p = pl.program_id(1)
    i = pl.program_id(2)
    q0 = i * tq
    n_full = q0 // tk
    qt = q_ref[0].astype(F32).T
    row = lax.broadcasted_iota(jnp.int32, (PAIR, 1), 0)
    qts, cqs = [], []
    for hh in range(2):
        h = 2 * p + hh
        in_head = (row < HEAD_DIM) if hh == 0 else (row >= HEAD_DIM)
        q_rows = jnp.where(in_head, qt, 0.0)
        pick = (row == h) | (row == h + N_HEADS) | (row == h + 2 * N_HEADS)
        one_rows = jnp.broadcast_to(jnp.where(pick, 1.0, 0.0), (PAIR, tq))
        qts.append(jnp.concatenate([q_rows, one_rows], axis=0).astype(BF16))
        cqs.append(cqt_ref[0, pl.ds(h, 1), :])
    qt_both = jnp.concatenate(qts, axis=1)
    cq = jnp.concatenate(cqs, axis=1)

    qt_ref[...] = qt_both
    m_ref[...] = jnp.full((1, 2 * tq), NEG, F32)
    acc_ref[...] = jnp.zeros((2, VT_ROWS, tq), F32)

    def qk(j, st_ref, cm_ref):
        ks = pl.multiple_of(j * tk, tk)
        kf = jnp.concatenate([k_ref[0, pl.ds(ks, tk), :], fs_ref[0, pl.ds(ks, tk), :]], axis=1)
        st = _dot(kf, qt_ref[...])
        st_ref[...] = st
        cm_ref[...] = jnp.max(st, axis=0, keepdims=True)

    def soft_pv(j, st_ref, cm_ref):
        ks = pl.multiple_of(j * tk, tk)
        m = m_ref[...]
        m_new = jnp.maximum(m, cm_ref[...] + cq)
        alpha = jnp.exp2(m - m_new)
        pb = jnp.exp2(st_ref[...] - (m_new - cq)).astype(BF16)
        m_ref[...] = m_new
        acc_ref[0] = alpha[:, :tq] * acc_ref[0] + _dot(vt_ref[0, 0, :, pl.ds(ks, tk)], pb[:, :tq])
        acc_ref[1] = alpha[:, tq:] * acc_ref[1] + _dot(vt_ref[0, 1, :, pl.ds(ks, tk)], pb[:, tq:])

    half = tq // 2
    ks_diag = pl.multiple_of(n_full * tk, tk)

    def qk_diag(st_ref):
        kf = jnp.concatenate([k_ref[0, pl.ds(ks_diag, tk), :], fs_ref[0, pl.ds(ks_diag, tk), :]], axis=1)
        qt_all = qt_ref[...]
        st_ref[0:half, :] = _dot(kf[0:half], qt_all)
        qt_hi = jnp.concatenate([qt_all[:, half:tq], qt_all[:, tq + half:]], axis=1)
        hi = _dot(kf[half:], qt_hi)
        st_ref[half:, half:tq] = hi[:, 0:half]
        st_ref[half:, tq + half:] = hi[:, half:]

    def soft_pv_diag(st_ref):
        causal = (lax.broadcasted_iota(jnp.int32, (half, half), 0)
                  <= lax.broadcasted_iota(jnp.int32, (half, half), 1))
        for hh in range(2):
            for c in range(2):
                cols = slice(hh * tq + c * half, hh * tq + (c + 1) * half)
                diag = jnp.where(causal, st_ref[c * half:(c + 1) * half, cols], NEG)
                parts = [diag] if c == 0 else [st_ref[0:half, cols], diag]
                cm = functools.reduce(jnp.maximum, [jnp.max(part, axis=0, keepdims=True) for part in parts])
                m = m_ref[:, cols]
                cq_c = cq[:, cols]
                m_new = jnp.maximum(m, cm + cq_c)
                alpha = jnp.exp2(m - m_new)
                shift = m_new - cq_c
                pb = jnp.concatenate([jnp.exp2(part - shift) for part in parts], axis=0).astype(BF16)
                m_ref[:, cols] = m_new
                out_cols = slice(c * half, (c + 1) * half)
                acc_ref[hh, :, out_cols] = (alpha * acc_ref[hh, :, out_cols]
                                            + _dot(vt_ref[0, hh, :, pl.ds(ks_diag, (c + 1) * half)], pb))

    qk(0, sta_ref, cma_ref)

    def pair(jj, c):
        qk(2 * jj + 1, stb_ref, cmb_ref)
        soft_pv(2 * jj, sta_ref, cma_ref)
        qk(2 * jj + 2, sta_ref, cma_ref)
        soft_pv(2 * jj + 1, stb_ref, cmb_ref)
        return c

    lax.fori_loop(0, n_full // 2, pair, 0)

    @pl.when(n_full % 2 == 1)
    def _():
        qk_diag(stb_ref)
        soft_pv(n_full - 1, sta_ref, cma_ref)
        soft_pv_diag(stb_ref)

    @pl.when(n_full % 2 == 0)
    def _():
        soft_pv_diag(sta_ref)

    ot = jnp.concatenate([acc_ref[hh, 0:HEAD_DIM, :] * (1.0 / acc_ref[hh, HEAD_DIM:HEAD_DIM + 1, :])
                          for hh in range(2)], axis=0)
    o_ref[0] = ot.T.astype(BF16)


def _fox(proj3, fs, vt, cqt):
    b, s, _ = proj3.shape
    tq, tk = TQ_FOX, TK_FOX
    assert tk == tq and s % tk == 0
    return pl.pallas_call(
        functools.partial(_fox_kernel, tq=tq, tk=tk),
        grid=(b, N_PAIRS, s // tq),
        in_specs=[
            pl.BlockSpec((1, tq, PAIR), lambda bi, p, i: (bi, i, p)),
            pl.BlockSpec((1, s, PAIR), lambda bi, p, i: (bi, 0, N_PAIRS + p)),
            pl.BlockSpec((1, s, LANES), lambda bi, p, i: (bi, 0, 0)),
            pl.BlockSpec((1, 2, VT_ROWS, s), lambda bi, p, i: (bi, p, 0, 0)),
            pl.BlockSpec((1, N_HEADS, tq), lambda bi, p, i: (bi, 0, i)),
        ],
        out_specs=pl.BlockSpec((1, tq, PAIR), lambda bi, p, i: (bi, i, p)),
        out_shape=jax.ShapeDtypeStruct((b, s, WIDTH), BF16),
        scratch_shapes=[pltpu.VMEM((2 * PAIR, 2 * tq), BF16), pltpu.VMEM((tk, 2 * tq), F32),
                        pltpu.VMEM((tk, 2 * tq), F32), pltpu.VMEM((1, 2 * tq), F32),
                        pltpu.VMEM((1, 2 * tq), F32), pltpu.VMEM((1, 2 * tq), F32),
                        pltpu.VMEM((2, VT_ROWS, tq), F32)],
        compiler_params=_params("parallel", "parallel", "arbitrary"),
        name="fox_attn",
    )(proj3, proj3, fs, vt, cqt)


def _chunk_kernel(q_ref, k_ref, vt_ref, bias_a_ref, bias_b_ref, bias_ref, o_ref, *, tq, nblk, nsub):
    i = pl.program_id(2)
    w = nblk * tq
    qt = q_ref[0].astype(F32).T
    row = lax.broadcasted_iota(jnp.int32, (PAIR, 1), 0)
    head_rows = [row < HEAD_DIM, row >= HEAD_DIM]
    bias_refs = [bias_a_ref, bias_b_ref] + [bias_ref] * (nsub - 2)

    def scores(sb):
        g = nsub * i + sb
        ws = pl.multiple_of(jnp.maximum(g - (nblk - 1), 0) * tq, tq)
        qs = qt[:, sb * tq:(sb + 1) * tq]
        qt_both = jnp.concatenate([jnp.where(hr, qs, 0.0) for hr in head_rows], axis=1).astype(BF16)
        st = _dot(k_ref[0, pl.ds(ws, w), :], qt_both)
        return ws, [st[:, hh * tq:(hh + 1) * tq] + bias_refs[sb][0, 0, hh] for hh in range(2)]

    def finish(sb, ws, sts):
        cols = []
        for hh in range(2):
            st = sts[hh]
            pb = jnp.exp2(st - jnp.max(st, axis=0, keepdims=True)).astype(BF16)
            acc = _dot(vt_ref[0, hh, :, pl.ds(ws, w)], pb)
            cols.append(acc[0:HEAD_DIM] * (1.0 / acc[HEAD_DIM:HEAD_DIM + 1]))
        o_ref[0, sb * tq:(sb + 1) * tq, :] = jnp.concatenate(cols, axis=0).T.astype(BF16)

    pending = scores(0)
    for sb in range(nsub):
        nxt = scores(sb + 1) if sb + 1 < nsub else None
        finish(sb, *pending)
        pending = nxt


def _chunk_bias_tables(rel_bias, tq):
    left = LEFT_CHUNKS * CHUNK
    w = tq + left
    row_len = 1 << (tq + w - 2).bit_length()
    period = row_len + 1
    lead = rel_bias.shape[:-1]
    top = rel_bias[..., 2 * REL_CLIP:]
    bot = rel_bias[..., :1]
    n_top = left - REL_CLIP + 1
    n_bot = w - n_top - 2 * REL_CLIP
    u = jnp.concatenate([
        jnp.broadcast_to(top, lead + (n_top,)),
        rel_bias[..., :2 * REL_CLIP][..., ::-1],
        jnp.broadcast_to(bot, lead + (n_bot + period - w - (tq - 1),)),
        jnp.broadcast_to(top, lead + (tq - 1,)),
    ], axis=-1).astype(F32)
    flat = jnp.tile(u, (1,) * len(lead) + (tq + 1,))[..., :tq * row_len]
    bias = flat.reshape(lead + (tq, row_len))[..., :w]
    qc = np.arange(tq)[:, None] // CHUNK
    kc = np.arange(w)[None, :] // CHUNK
    band = (kc >= qc) & (kc <= qc + LEFT_CHUNKS)
    full = jnp.swapaxes(jnp.where(jnp.asarray(band), bias * LOG2E, NEG), -1, -2)
    gone = jnp.full(lead + (tq, tq), NEG, F32)
    v1 = jnp.concatenate([full[..., tq:, :], gone], axis=-2)
    v0 = jnp.concatenate([full[..., 2 * tq:, :], gone, gone], axis=-2)
    return jnp.stack([v0, v1, full], axis=1)


def _chunk_attn(proj3, vtc, bias_tables, layer):
    b, s, _ = proj3.shape
    tq = TQ_CHUNK
    left = LEFT_CHUNKS * CHUNK
    assert left == 2 * tq and s >= 3 * tq
    nblk = 3
    nsub = NSUB_CHUNK
    assert nsub >= 2 and s % (nsub * tq) == 0
    col0 = 3 * N_PAIRS
    return pl.pallas_call(
        functools.partial(_chunk_kernel, tq=tq, nblk=nblk, nsub=nsub),
        grid=(b, N_PAIRS, s // (nsub * tq)),
        in_specs=[
            pl.BlockSpec((1, nsub * tq, PAIR), lambda bi, p, i: (bi, i, col0 + p)),
            pl.BlockSpec((1, s, PAIR), lambda bi, p, i: (bi, 0, col0 + N_PAIRS + p)),
            pl.BlockSpec((1, 2, VT_ROWS, s), lambda bi, p, i: (bi, p, 0, 0)),
            pl.BlockSpec((1, 1, 2, nblk * tq, tq), lambda bi, p, i: (layer, jnp.minimum(nsub * i, 2), p, 0, 0)),
            pl.BlockSpec((1, 1, 2, nblk * tq, tq), lambda bi, p, i: (layer, jnp.minimum(nsub * i + 1, 2), p, 0, 0)),
            pl.BlockSpec((1, 1, 2, nblk * tq, tq), lambda bi, p, i: (layer, 2, p, 0, 0)),
        ],
        out_specs=pl.BlockSpec((1, nsub * tq, PAIR), lambda bi, p, i: (bi, i, p)),
        out_shape=jax.ShapeDtypeStruct((b, s, WIDTH), BF16),
        compiler_params=_params("parallel", "parallel", "arbitrary"),
        name="chunk_attn",
    )(proj3, proj3, vtc, bias_tables, bias_tables, bias_tables)


def _mix_kernel(oa_ref, oc_ref, ga_ref, gc_ref, x_ref, mod_ref, wa_ref, wc_ref, wo_ref,
                g_ref, b_ref, o_ref, *, alpha):
    ya = _dot(oa_ref[...], wa_ref[...])
    yc = _dot(oc_ref[...], wc_ref[...])
    merged = (jax.nn.sigmoid(ga_ref[...].astype(F32)) * ya
              + jax.nn.sigmoid(gc_ref[...].astype(F32)) * yc)
    mix = _dot(merged.astype(BF16), wo_ref[...])
    z = alpha * x_ref[...] + (1.0 + mod_ref[0, 2:3, :]) * mix
    o_ref[...] = _ln(z) * g_ref[...] + b_ref[...]


def _mix(oa, oc, proj, xr, mod, w_a, w_c, w_o, gain, bias, seq, alpha):
    rows, d = xr.shape
    tm = TM_MIX
    tiles_per_seq = seq // tm
    gate_blk = (2 * 3 * WIDTH) // d
    const = lambda i: (0, 0)
    return pl.pallas_call(
        functools.partial(_mix_kernel, alpha=alpha),
        grid=(rows // tm,),
        in_specs=[
            pl.BlockSpec((tm, WIDTH), lambda i: (i, 0)),
            pl.BlockSpec((tm, WIDTH), lambda i: (i, 0)),
            pl.BlockSpec((tm, d), lambda i: (i, gate_blk)),
            pl.BlockSpec((tm, d), lambda i: (i, gate_blk + 1)),
            pl.BlockSpec((tm, d), lambda i: (i, 0)),
            pl.BlockSpec((1, N_MOD, d), lambda i: (i // tiles_per_seq, 0, 0)),
            pl.BlockSpec((WIDTH, d), const),
            pl.BlockSpec((WIDTH, d), const),
            pl.BlockSpec((d, d), const),
            pl.BlockSpec((1, d), const),
            pl.BlockSpec((1, d), const),
        ],
        out_specs=pl.BlockSpec((tm, d), lambda i: (i, 0)),
        out_shape=jax.ShapeDtypeStruct((rows, d), F32),
        compiler_params=_params("parallel"),
        name="mix_out",
    )(oa, oc, proj, proj, xr, mod, w_a, w_c, w_o, gain, bias)


def _ffn_kernel(x_ref, xh_ref, mod_ref, wu_ref, cw_ref, cb_ref, wd_ref, g_ref, b_ref, o_ref, h_ref, acc_ref,
                *, alpha, tm, ck):
    t = pl.program_id(1)
    d_ff = wd_ref.shape[0]
    x = x_ref[0]
    sc = 1.0 + mod_ref[0, 4:5, :]
    sh = mod_ref[0, 3:4, :]
    h_ref[HALO:, :] = (_ln(x) * sc + sh).astype(BF16)
    keep = jnp.where(t > 0, 1.0, 0.0)
    h_ref[0:HALO, :] = ((_ln(xh_ref[0]) * sc + sh) * keep).astype(BF16)
    acc_ref[...] = jnp.zeros_like(acc_ref)

    def conv(u, lo_col):
        cols = slice(lo_col, lo_col + ck)
        lo = HALO - (CONV_WIDTH - 1)
        y = cb_ref[:, cols]
        for j in range(CONV_WIDTH):
            y = y + cw_ref[j:j + 1, cols] * u[lo + j:lo + j + tm, :]
        return y

    def up(c):
        he = h_ref[...]
        return (_dot(he, wu_ref[:, c * ck:(c + 1) * ck]),
                _dot(he, wu_ref[:, d_ff + c * ck:d_ff + (c + 1) * ck]))

    u = up(0)
    for c in range(d_ff // ck):
        u_next = up(c + 1) if (c + 1) * ck < d_ff else None
        a = conv(u[0], c * ck)
        v = conv(u[1], d_ff + c * ck)
        act = (a * jax.nn.sigmoid(a) * v).astype(BF16)
        acc_ref[...] += _dot(act, wd_ref[c * ck:(c + 1) * ck, :])
        u = u_next
    z = alpha * x + (1.0 + mod_ref[0, 5:6, :]) * acc_ref[...]
    o_ref[0] = _ln(z) * g_ref[...] + b_ref[...]


def _ffn(x3, mod, w_up, conv_w, conv_b, w_down, gain, bias, alpha):
    b, s, d = x3.shape
    tm = TM_FFN
    d_ff = w_down.shape[0]
    assert d_ff % CK_FFN == 0
    halo_per_tile = tm // HALO
    c2 = lambda bi, t: (0, 0)
    return pl.pallas_call(
        functools.partial(_ffn_kernel, alpha=alpha, tm=tm, ck=CK_FFN),
        grid=(b, s // tm),
        in_specs=[
            pl.BlockSpec((1, tm, d), lambda bi, t: (bi, t, 0)),
            pl.BlockSpec((1, HALO, d), lambda bi, t: (bi, jnp.maximum(t * halo_per_tile - 1, 0), 0)),
            pl.BlockSpec((1, N_MOD, d), lambda bi, t: (bi, 0, 0)),
            pl.BlockSpec((d, 2 * d_ff), c2),
            pl.BlockSpec((CONV_WIDTH, 2 * d_ff), c2),
            pl.BlockSpec((1, 2 * d_ff), c2),
            pl.BlockSpec((d_ff, d), c2),
            pl.BlockSpec((1, d), c2),
            pl.BlockSpec((1, d), c2),
        ],
        out_specs=pl.BlockSpec((1, tm, d), lambda bi, t: (bi, t, 0)),
        out_shape=jax.ShapeDtypeStruct((b, s, d), F32),
        scratch_shapes=[pltpu.VMEM((tm + HALO, d), BF16), pltpu.VMEM((tm, d), F32)],
        compiler_params=_params("parallel", "parallel"),
        name="conv_ffn",
    )(x3, x3, mod, w_up, conv_w, conv_b, w_down, gain, bias)


def kernel(x, c, w_in, b_f, rel_bias, w_br_fox, w_br_chunk, w_out, w_up, conv_w, conv_b, w_down,
           w_ada, b_ada, ln1_g, ln1_b, ln2_g, ln2_b):
    b, s, d = x.shape
    depth = w_in.shape[0]
    alpha = (2.0 * depth) ** 0.25
    rows = b * s
    mod_all = _ada_mod(c, w_ada, b_ada).reshape(depth, b, N_MOD, d)
    bias_tables = _chunk_bias_tables(rel_bias, TQ_CHUNK)

    f0 = 3 * WIDTH
    f1 = f0 + N_HEADS
    xr = x.reshape(rows, d)
    for l in range(depth):
        mod = mod_all[l]
        q_scale = LOG2E / math.sqrt(HEAD_DIM)
        qc0 = f1
        qc1 = f1 + WIDTH
        w_main = jnp.concatenate([w_in[l, :, :WIDTH] * q_scale, w_in[l, :, WIDTH:f0],
                                  w_in[l, :, qc0:qc1] * q_scale, w_in[l, :, qc1:]], axis=1).astype(BF16)
        w_f = jnp.pad(w_in[l, :, f0:f1], ((0, 0), (0, LANES - N_HEADS))).astype(BF16)
        bf_pad = jnp.pad(b_f[l], (0, LANES - N_HEADS)).reshape(1, LANES)

        proj, f = _ln_proj(xr, mod, w_main, w_f, s)
        proj3 = proj.reshape(b, s, MAIN_COLS)
        cqt, fs, vt, vtc = _attn_prep(f.reshape(b, s, LANES), bf_pad, proj3)
        o_a = _fox(proj3, fs, vt, cqt)
        o_c = _chunk_attn(proj3, vtc, bias_tables, l)
        xr = _mix(o_a.reshape(rows, WIDTH), o_c.reshape(rows, WIDTH), proj, xr, mod,
                  w_br_fox[l].astype(BF16), w_br_chunk[l].astype(BF16), w_out[l].astype(BF16),
                  ln1_g[l].reshape(1, d), ln1_b[l].reshape(1, d), s, alpha)

        x3 = _ffn(xr.reshape(b, s, d), mod, w_up[l].astype(BF16), conv_w[l], conv_b[l].reshape(1, -1),
                  w_down[l].astype(BF16), ln2_g[l].reshape(1, d), ln2_b[l].reshape(1, d), alpha)
        xr = x3.reshape(rows, d)
    return xr.reshape(b, s, d)
```

```python
import functools
import math

import numpy as np
import jax
import jax.numpy as jnp
from jax import lax
from jax.experimental import pallas as pl
from jax.experimental.pallas import tpu as pltpu

F32 = jnp.float32
BF16 = jnp.bfloat16

D_MODEL = 1024
HEAD_DIM = 64
N_HEADS = 8
WIDTH = N_HEADS * HEAD_DIM
CHUNK = 64
LEFT_CHUNKS = 8
REL_CLIP = 128
CONV_WIDTH = 3
LN_EPS = 1e-5
N_MOD = 6
LANES = 128
PAIR = 2 * HEAD_DIM
N_PAIRS = N_HEADS // 2
MAIN_COLS = 3 * WIDTH + 3 * WIDTH + 2 * D_MODEL
LOG2E = math.log2(math.e)
NEG = -1e30
VMEM_LIMIT = 56 * 1024 * 1024

TM_PROJ = 1024
TN_PROJ = 5120
T_CUM = 512
TQ_FOX = 1024
TK_FOX = 1024
TQ_CHUNK = 256
NSUB_CHUNK = 16
TM_MIX = 1024
TM_FFN = 512
CK_FFN = 2816
VT_ROWS = HEAD_DIM + 16
HALO = 16


def _ln(x):
    mu = jnp.mean(x, axis=-1, keepdims=True)
    xc = x - mu
    var = jnp.mean(xc * xc, axis=-1, keepdims=True)
    return xc * lax.rsqrt(var + LN_EPS)


def _dot(a, b):
    return jnp.dot(a, b, preferred_element_type=F32)


def _split3(v):
    hi = v.astype(BF16)
    r = v - hi.astype(F32)
    mid = r.astype(BF16)
    lo = (r - mid.astype(F32)).astype(BF16)
    return hi, mid, lo


def _params(*sem):
    return pltpu.CompilerParams(dimension_semantics=sem, vmem_limit_bytes=VMEM_LIMIT)


def _mod_kernel(c_ref, w_ref, b_ref, o_ref):
    c = c_ref[...]
    cond = c * jax.nn.sigmoid(c)
    a_hi, a_mid, _ = _split3(cond)
    w = w_ref[0]
    w_hi = w.astype(BF16)
    w_lo = (w - w_hi.astype(F32)).astype(BF16)
    acc = _dot(a_hi, w_hi) + _dot(a_mid, w_hi) + _dot(a_hi, w_lo)
    o_ref[0] = acc + b_ref[0]


def _ada_mod(c, w_ada, b_ada):
    depth, d, n = w_ada.shape
    b = c.shape[0]
    rows = 8
    c_pad = jnp.pad(c, ((0, rows - b), (0, 0)))
    tn = 1024
    out = pl.pallas_call(
        _mod_kernel,
        grid=(depth, n // tn),
        in_specs=[
            pl.BlockSpec((rows, d), lambda l, j: (0, 0)),
            pl.BlockSpec((1, d, tn), lambda l, j: (l, 0, j)),
            pl.BlockSpec((1, 1, tn), lambda l, j: (l, 0, j)),
        ],
        out_specs=pl.BlockSpec((1, rows, tn), lambda l, j: (l, 0, j)),
        out_shape=jax.ShapeDtypeStruct((depth, rows, n), F32),
        compiler_params=_params("parallel", "parallel"),
        name="ada_mod",
    )(c_pad, w_ada, b_ada.reshape(depth, 1, n))
    return out[:, :b, :]


def _ln_proj_kernel(x_ref, mod_ref, w_ref, wf_ref, o_ref, f_ref, h_ref):
    @pl.when(pl.program_id(1) == 0)
    def _():
        y = _ln(x_ref[...])
        h = (y * (1.0 + mod_ref[0, 1:2, :]) + mod_ref[0, 0:1, :]).astype(BF16)
        h_ref[...] = h
        f_ref[...] = _dot(h, wf_ref[...])

    o_ref[...] = _dot(h_ref[...], w_ref[...]).astype(BF16)


def _ln_proj(xr, mod, w_main, w_f, seq):
    rows, d = xr.shape
    tm, tn = TM_PROJ, TN_PROJ
    n = w_main.shape[1]
    tiles_per_seq = seq // tm
    return pl.pallas_call(
        _ln_proj_kernel,
        grid=(rows // tm, n // tn),
        in_specs=[
            pl.BlockSpec((tm, d), lambda i, j: (i, 0)),
            pl.BlockSpec((1, N_MOD, d), lambda i, j: (i // tiles_per_seq, 0, 0)),
            pl.BlockSpec((d, tn), lambda i, j: (0, j),
                         pipeline_mode=pl.Buffered(1) if n == tn else None),
            pl.BlockSpec((d, LANES), lambda i, j: (0, 0), pipeline_mode=pl.Buffered(1)),
        ],
        out_specs=[
            pl.BlockSpec((tm, tn), lambda i, j: (i, j)),
            pl.BlockSpec((tm, LANES), lambda i, j: (i, 0)),
        ],
        out_shape=[
            jax.ShapeDtypeStruct((rows, n), BF16),
            jax.ShapeDtypeStruct((rows, LANES), F32),
        ],
        scratch_shapes=[pltpu.VMEM((tm, d), BF16)],
        compiler_params=_params("parallel", "arbitrary"),
        name="ln_proj",
    )(xr, mod, w_main, w_f)


def _attn_prep_kernel(f_ref, bf_ref, v_ref, vc_ref, cqt_ref, fs_ref, vt_ref, vtc_ref, carry_ref):
    @pl.when(pl.program_id(1) == 0)
    def _():
        carry_ref[...] = jnp.zeros_like(carry_ref)

    z = f_ref[0] + bf_ref[...]
    lf = jnp.minimum(z, 0.0) - jnp.log(1.0 + jnp.exp(-jnp.abs(z)))
    hi, mid, lo = _split3(lf)
    tc = lf.shape[0]
    row = lax.broadcasted_iota(jnp.int32, (tc, tc), 0)
    col = lax.broadcasted_iota(jnp.int32, (tc, tc), 1)
    tri = jnp.where(col <= row, 1.0, 0.0).astype(BF16)
    cum = _dot(tri, hi) + _dot(tri, mid) + _dot(tri, lo) + carry_ref[...]
    carry_ref[...] = cum[tc - 1:tc, :]

    cum2 = cum * LOG2E
    cqt_ref[0] = cum2.T[0:N_HEADS, :]
    neg = -cum2
    n_hi = neg.astype(BF16).astype(F32)
    r1 = neg - n_hi
    n_mid = r1.astype(BF16).astype(F32)
    n_lo = r1 - n_mid
    lane = lax.broadcasted_iota(jnp.int32, (1, LANES), 1)
    placed = jnp.where(lane < N_HEADS, n_hi,
                       jnp.where(lane < 2 * N_HEADS, pltpu.roll(n_mid, N_HEADS, axis=1),
                                 jnp.where(lane < 3 * N_HEADS, pltpu.roll(n_lo, 2 * N_HEADS, axis=1), 0.0)))
    fs_ref[0] = placed.astype(BF16)

    extra = VT_ROWS - HEAD_DIM
    ones_row = jnp.where(lax.broadcasted_iota(jnp.int32, (extra, tc), 0) == 0, 1.0, 0.0)
    for src_ref, dst_ref in ((v_ref, vt_ref), (vc_ref, vtc_ref)):
        for p in range(N_PAIRS):
            vt = src_ref[0, :, p * PAIR:(p + 1) * PAIR].astype(F32).T
            dst_ref[0, 2 * p] = jnp.concatenate([vt[0:HEAD_DIM], ones_row], axis=0).astype(BF16)
            dst_ref[0, 2 * p + 1] = jnp.concatenate([vt[HEAD_DIM:], ones_row], axis=0).astype(BF16)


def _attn_prep(f3, bf_pad, proj3):
    b, s, _ = f3.shape
    tc = T_CUM
    v_blk = 2
    vc_blk = 5
    return pl.pallas_call(
        _attn_prep_kernel,
        grid=(b, s // tc),
        in_specs=[
            pl.BlockSpec((1, tc, LANES), lambda bi, t: (bi, t, 0)),
            pl.BlockSpec((1, LANES), lambda bi, t: (0, 0)),
            pl.BlockSpec((1, tc, WIDTH), lambda bi, t: (bi, t, v_blk)),
            pl.BlockSpec((1, tc, WIDTH), lambda bi, t: (bi, t, vc_blk)),
        ],
        out_specs=[
            pl.BlockSpec((1, N_HEADS, tc), lambda bi, t: (bi, 0, t)),
            pl.BlockSpec((1, tc, LANES), lambda bi, t: (bi, t, 0)),
            pl.BlockSpec((1, N_HEADS, VT_ROWS, tc), lambda bi, t: (bi, 0, 0, t)),
            pl.BlockSpec((1, N_HEADS, VT_ROWS, tc), lambda bi, t: (bi, 0, 0, t)),
        ],
        out_shape=[
            jax.ShapeDtypeStruct((b, N_HEADS, s), F32),
            jax.ShapeDtypeStruct((b, s, LANES), BF16),
            jax.ShapeDtypeStruct((b, N_HEADS, VT_ROWS, s), BF16),
            jax.ShapeDtypeStruct((b, N_HEADS, VT_ROWS, s), BF16),
        ],
        scratch_shapes=[pltpu.VMEM((1, LANES), F32)],
        compiler_params=_params("parallel", "arbitrary"),
        name="attn_prep",
    )(f3, bf_pad, proj3, proj3)


def _fox_kernel(q_ref, k_ref, fs_ref, vt_ref, cqt_ref, o_ref, qt_ref, sta_ref, stb_ref, cma_ref, cmb_ref,
                m_ref, acc_ref, *, tq, tk):
    p = pl.program_id(1)
    i = pl.program_id(2)
    q0 = i * tq
    n_full = q0 // tk
    qt = q_ref[0].astype(F32).T
    row = lax.broadcasted_iota(jnp.int32, (PAIR, 1), 0)
    qts, cqs = [], []
    for hh in range(2):
        h = 2 * p + hh
        in_head = (row < HEAD_DIM) if hh == 0 else (row >= HEAD_DIM)
        q_rows = jnp.where(in_head, qt, 0.0)
        pick = (row == h) | (row == h + N_HEADS) | (row == h + 2 * N_HEADS)
        one_rows = jnp.broadcast_to(jnp.where(pick, 1.0, 0.0), (PAIR, tq))
        qts.append(jnp.concatenate([q_rows, one_rows], axis=0).astype(BF16))
        cqs.append(cqt_ref[0, pl.ds(h, 1), :])
    qt_both = jnp.concatenate(qts, axis=1)
    cq = jnp.concatenate(cqs, axis=1)

    qt_ref[...] = qt_both
    m_ref[...] = jnp.full((1, 2 * tq), NEG, F32)
    acc_ref[...] = jnp.zeros((2, VT_ROWS, tq), F32)

    def qk(j, st_ref, cm_ref):
        ks = pl.multiple_of(j * tk, tk)
        kf = jnp.concatenate([k_ref[0, pl.ds(ks, tk), :], fs_ref[0, pl.ds(ks, tk), :]], axis=1)
        st = _dot(kf, qt_ref[...])
        st_ref[...] = st
        cm_ref[...] = jnp.max(st, axis=0, keepdims=True)

    def soft_pv(j, st_ref, cm_ref):
        ks = pl.multiple_of(j * tk, tk)
        m = m_ref[...]
        m_new = jnp.maximum(m, cm_ref[...] + cq)
        alpha = jnp.exp2(m - m_new)
        pb = jnp.exp2(st_ref[...] - (m_new - cq)).astype(BF16)
        m_ref[...] = m_new
        acc_ref[0] = alpha[:, :tq] * acc_ref[0] + _dot(vt_ref[0, 0, :, pl.ds(ks, tk)], pb[:, :tq])
        acc_ref[1] = alpha[:, tq:] * acc_ref[1] + _dot(vt_ref[0, 1, :, pl.ds(ks, tk)], pb[:, tq:])

    half = tq // 2
    ks_diag = pl.multiple_of(n_full * tk, tk)

    def qk_diag(st_ref):
        kf = jnp.concatenate([k_ref[0, pl.ds(ks_diag, tk), :], fs_ref[0, pl.ds(ks_diag, tk), :]], axis=1)
        qt_all = qt_ref[...]
        st_ref[0:half, :] = _dot(kf[0:half], qt_all)
        qt_hi = jnp.concatenate([qt_all[:, half:tq], qt_all[:, tq + half:]], axis=1)
        hi = _dot(kf[half:], qt_hi)
        st_ref[half:, half:tq] = hi[:, 0:half]
        st_ref[half:, tq + half:] = hi[:, half:]

    def soft_pv_diag(st_ref):
        causal = (lax.broadcasted_iota(jnp.int32, (half, half), 0)
                  <= lax.broadcasted_iota(jnp.int32, (half, half), 1))
        for hh in range(2):
            for c in range(2):
                cols = slice(hh * tq + c * half, hh * tq + (c + 1) * half)
                diag = jnp.where(causal, st_ref[c * half:(c + 1) * half, cols], NEG)
                parts = [diag] if c == 0 else [st_ref[0:half, cols], diag]
                cm = functools.reduce(jnp.maximum, [jnp.max(part, axis=0, keepdims=True) for part in parts])
                m = m_ref[:, cols]
                cq_c = cq[:, cols]
                m_new = jnp.maximum(m, cm + cq_c)
                alpha = jnp.exp2(m - m_new)
                shift = m_new - cq_c
                pb = jnp.concatenate([jnp.exp2(part - shift) for part in parts], axis=0).astype(BF16)
                m_ref[:, cols] = m_new
                out_cols = slice(c * half, (c + 1) * half)
                acc_ref[hh, :, out_cols] = (alpha * acc_ref[hh, :, out_cols]
                                            + _dot(vt_ref[0, hh, :, pl.ds(ks_diag, (c + 1) * half)], pb))

    qk(0, sta_ref, cma_ref)

    def pair(jj, c):
        qk(2 * jj + 1, stb_ref, cmb_ref)
        soft_pv(2 * jj, sta_ref, cma_ref)
        qk(2 * jj + 2, sta_ref, cma_ref)
        soft_pv(2 * jj + 1, stb_ref, cmb_ref)
        return c

    lax.fori_loop(0, n_full // 2, pair, 0)

    @pl.when(n_full % 2 == 1)
    def _():
        qk_diag(stb_ref)
        soft_pv(n_full - 1, sta_ref, cma_ref)
        soft_pv_diag(stb_ref)

    @pl.when(n_full % 2 == 0)
    def _():
        soft_pv_diag(sta_ref)

    ot = jnp.concatenate([acc_ref[hh, 0:HEAD_DIM, :] * (1.0 / acc_ref[hh, HEAD_DIM:HEAD_DIM + 1, :])
                          for hh in range(2)], axis=0)
    o_ref[0] = ot.T.astype(BF16)


def _fox(proj3, fs, vt, cqt):
    b, s, _ = proj3.shape
    tq, tk = TQ_FOX, TK_FOX
    assert tk == tq and s % tk == 0
    return pl.pallas_call(
        functools.partial(_fox_kernel, tq=tq, tk=tk),
        grid=(b, N_PAIRS, s // tq),
        in_specs=[
            pl.BlockSpec((1, tq, PAIR), lambda bi, p, i: (bi, i, p)),
            pl.BlockSpec((1, s, PAIR), lambda bi, p, i: (bi, 0, N_PAIRS + p)),
            pl.BlockSpec((1, s, LANES), lambda bi, p, i: (bi, 0, 0)),
            pl.BlockSpec((1, 2, VT_ROWS, s), lambda bi, p, i: (bi, p, 0, 0)),
            pl.BlockSpec((1, N_HEADS, tq), lambda bi, p, i: (bi, 0, i)),
        ],
        out_specs=pl.BlockSpec((1, tq, PAIR), lambda bi, p, i: (bi, i, p)),
        out_shape=jax.ShapeDtypeStruct((b, s, WIDTH), BF16),
        scratch_shapes=[pltpu.VMEM((2 * PAIR, 2 * tq), BF16), pltpu.VMEM((tk, 2 * tq), F32),
                        pltpu.VMEM((tk, 2 * tq), F32), pltpu.VMEM((1, 2 * tq), F32),
                        pltpu.VMEM((1, 2 * tq), F32), pltpu.VMEM((1, 2 * tq), F32),
                        pltpu.VMEM((2, VT_ROWS, tq), F32)],
        compiler_params=_params("parallel", "parallel", "arbitrary"),
        name="fox_attn",
    )(proj3, proj3, fs, vt, cqt)


def _chunk_kernel(q_ref, k_ref, vt_ref, bias_a_ref, bias_b_ref, bias_ref, o_ref, *, tq, nblk, nsub):
    i = pl.program_id(2)
    w = nblk * tq
    qt = q_ref[0].astype(F32).T
    row = lax.broadcasted_iota(jnp.int32, (PAIR, 1), 0)
    head_rows = [row < HEAD_DIM, row >= HEAD_DIM]
    bias_refs = [bias_a_ref, bias_b_ref] + [bias_ref] * (nsub - 2)

    def scores(sb):
        g = nsub * i + sb
        ws = pl.multiple_of(jnp.maximum(g - (nblk - 1), 0) * tq, tq)
        qs = qt[:, sb * tq:(sb + 1) * tq]
        qt_both = jnp.concatenate([jnp.where(hr, qs, 0.0) for hr in head_rows], axis=1).astype(BF16)
        st = _dot(k_ref[0, pl.ds(ws, w), :], qt_both)
        return ws, [st[:, hh * tq:(hh + 1) * tq] + bias_refs[sb][0, 0, hh] for hh in range(2)]

    def finish(sb, ws, sts):
        cols = []
        for hh in range(2):
            st = sts[hh]
            pb = jnp.exp2(st - jnp.max(st, axis=0, keepdims=True)).astype(BF16)
            acc = _dot(vt_ref[0, hh, :, pl.ds(ws, w)], pb)
            cols.append(acc[0:HEAD_DIM] * (1.0 / acc[HEAD_DIM:HEAD_DIM + 1]))
        o_ref[0, sb * tq:(sb + 1) * tq, :] = jnp.concatenate(cols, axis=0).T.astype(BF16)

    pending = scores(0)
    for sb in range(nsub):
        nxt = scores(sb + 1) if sb + 1 < nsub else None
        finish(sb, *pending)
        pending = nxt


def _chunk_bias_tables(rel_bias, tq):
    left = LEFT_CHUNKS * CHUNK
    w = tq + left
    row_len = 1 << (tq + w - 2).bit_length()
    period = row_len + 1
    lead = rel_bias.shape[:-1]
    top = rel_bias[..., 2 * REL_CLIP:]
    bot = rel_bias[..., :1]
    n_top = left - REL_CLIP + 1
    n_bot = w - n_top - 2 * REL_CLIP
    u = jnp.concatenate([
        jnp.broadcast_to(top, lead + (n_top,)),
        rel_bias[..., :2 * REL_CLIP][..., ::-1],
        jnp.broadcast_to(bot, lead + (n_bot + period - w - (tq - 1),)),
        jnp.broadcast_to(top, lead + (tq - 1,)),
    ], axis=-1).astype(F32)
    flat = jnp.tile(u, (1,) * len(lead) + (tq + 1,))[..., :tq * row_len]
    bias = flat.reshape(lead + (tq, row_len))[..., :w]
    qc = np.arange(tq)[:, None] // CHUNK
    kc = np.arange(w)[None, :] // CHUNK
    band = (kc >= qc) & (kc <= qc + LEFT_CHUNKS)
    full = jnp.swapaxes(jnp.where(jnp.asarray(band), bias * LOG2E, NEG), -1, -2)
    gone = jnp.full(lead + (tq, tq), NEG, F32)
    v1 = jnp.concatenate([full[..., tq:, :], gone], axis=-2)
    v0 = jnp.concatenate([full[..., 2 * tq:, :], gone, gone], axis=-2)
    return jnp.stack([v0, v1, full], axis=1)


def _chunk_attn(proj3, vtc, bias_tables, layer):
    b, s, _ = proj3.shape
    tq = TQ_CHUNK
    left = LEFT_CHUNKS * CHUNK
    assert left == 2 * tq and s >= 3 * tq
    nblk = 3
    nsub = NSUB_CHUNK
    assert nsub >= 2 and s % (nsub * tq) == 0
    col0 = 3 * N_PAIRS
    return pl.pallas_call(
        functools.partial(_chunk_kernel, tq=tq, nblk=nblk, nsub=nsub),
        grid=(b, N_PAIRS, s // (nsub * tq)),
        in_specs=[
            pl.BlockSpec((1, nsub * tq, PAIR), lambda bi, p, i: (bi, i, col0 + p)),
            pl.BlockSpec((1, s, PAIR), lambda bi, p, i: (bi, 0, col0 + N_PAIRS + p)),
            pl.BlockSpec((1, 2, VT_ROWS, s), lambda bi, p, i: (bi, p, 0, 0)),
            pl.BlockSpec((1, 1, 2, nblk * tq, tq), lambda bi, p, i: (layer, jnp.minimum(nsub * i, 2), p, 0, 0)),
            pl.BlockSpec((1, 1, 2, nblk * tq, tq), lambda bi, p, i: (layer, jnp.minimum(nsub * i + 1, 2), p, 0, 0)),
            pl.BlockSpec((1, 1, 2, nblk * tq, tq), lambda bi, p, i: (layer, 2, p, 0, 0)),
        ],
        out_specs=pl.BlockSpec((1, nsub * tq, PAIR), lambda bi, p, i: (bi, i, p)),
        out_shape=jax.ShapeDtypeStruct((b, s, WIDTH), BF16),
        compiler_params=_params("parallel", "parallel", "arbitrary"),
        name="chunk_attn",
    )(proj3, proj3, vtc, bias_tables, bias_tables, bias_tables)


def _mix_kernel(oa_ref, oc_ref, ga_ref, gc_ref, x_ref, mod_ref, wa_ref, wc_ref, wo_ref,
                g_ref, b_ref, o_ref, *, alpha):
    ya = _dot(oa_ref[...], wa_ref[...])
    yc = _dot(oc_ref[...], wc_ref[...])
    merged = (jax.nn.sigmoid(ga_ref[...].astype(F32)) * ya
              + jax.nn.sigmoid(gc_ref[...].astype(F32)) * yc)
    mix = _dot(merged.astype(BF16), wo_ref[...])
    z = alpha * x_ref[...] + (1.0 + mod_ref[0, 2:3, :]) * mix
    o_ref[...] = _ln(z) * g_ref[...] + b_ref[...]


def _mix(oa, oc, proj, xr, mod, w_a, w_c, w_o, gain, bias, seq, alpha):
    rows, d = xr.shape
    tm = TM_MIX
    tiles_per_seq = seq // tm
    gate_blk = (2 * 3 * WIDTH) // d
    const = lambda i: (0, 0)
    return pl.pallas_call(
        functools.partial(_mix_kernel, alpha=alpha),
        grid=(rows // tm,),
        in_specs=[
            pl.BlockSpec((tm, WIDTH), lambda i: (i, 0)),
            pl.BlockSpec((tm, WIDTH), lambda i: (i, 0)),
            pl.BlockSpec((tm, d), lambda i: (i, gate_blk)),
            pl.BlockSpec((tm, d), lambda i: (i, gate_blk + 1)),
            pl.BlockSpec((tm, d), lambda i: (i, 0)),
            pl.BlockSpec((1, N_MOD, d), lambda i: (i // tiles_per_seq, 0, 0)),
            pl.BlockSpec((WIDTH, d), const),
            pl.BlockSpec((WIDTH, d), const),
            pl.BlockSpec((d, d), const),
            pl.BlockSpec((1, d), const),
            pl.BlockSpec((1, d), const),
        ],
        out_specs=pl.BlockSpec((tm, d), lambda i: (i, 0)),
        out_shape=jax.ShapeDtypeStruct((rows, d), F32),
        compiler_params=_params("parallel"),
        name="mix_out",
    )(oa, oc, proj, proj, xr, mod, w_a, w_c, w_o, gain, bias)


def _ffn_kernel(x_ref, xh_ref, mod_ref, wu_ref, cw_ref, cb_ref, wd_ref, g_ref, b_ref, o_ref, h_ref, acc_ref,
                *, alpha, tm, ck):
    t = pl.program_id(1)
    d_ff = wd_ref.shape[0]
    x = x_ref[0]
    sc = 1.0 + mod_ref[0, 4:5, :]
    sh = mod_ref[0, 3:4, :]
    h_ref[HALO:, :] = (_ln(x) * sc + sh).astype(BF16)
    keep = jnp.where(t > 0, 1.0, 0.0)
    h_ref[0:HALO, :] = ((_ln(xh_ref[0]) * sc + sh) * keep).astype(BF16)
    acc_ref[...] = jnp.zeros_like(acc_ref)

    def conv(u, lo_col):
        cols = slice(lo_col, lo_col + ck)
        lo = HALO - (CONV_WIDTH - 1)
        y = cb_ref[:, cols]
        for j in range(CONV_WIDTH):
            y = y + cw_ref[j:j + 1, cols] * u[lo + j:lo + j + tm, :]
        return y

    def up(c):
        he = h_ref[...]
        return (_dot(he, wu_ref[:, c * ck:(c + 1) * ck]),
                _dot(he, wu_ref[:, d_ff + c * ck:d_ff + (c + 1) * ck]))

    u = up(0)
    for c in range(d_ff // ck):
        u_next = up(c + 1) if (c + 1) * ck < d_ff else None
        a = conv(u[0], c * ck)
        v = conv(u[1], d_ff + c * ck)
        act = (a * jax.nn.sigmoid(a) * v).astype(BF16)
        acc_ref[...] += _dot(act, wd_ref[c * ck:(c + 1) * ck, :])
        u = u_next
    z = alpha * x + (1.0 + mod_ref[0, 5:6, :]) * acc_ref[...]
    o_ref[0] = _ln(z) * g_ref[...] + b_ref[...]


def _ffn(x3, mod, w_up, conv_w, conv_b, w_down, gain, bias, alpha):
    b, s, d = x3.shape
    tm = TM_FFN
    d_ff = w_down.shape[0]
    assert d_ff % CK_FFN == 0
    halo_per_tile = tm // HALO
    c2 = lambda bi, t: (0, 0)
    return pl.pallas_call(
        functools.partial(_ffn_kernel, alpha=alpha, tm=tm, ck=CK_FFN),
        grid=(b, s // tm),
        in_specs=[
            pl.BlockSpec((1, tm, d), lambda bi, t: (bi, t, 0)),
            pl.BlockSpec((1, HALO, d), lambda bi, t: (bi, jnp.maximum(t * halo_per_tile - 1, 0), 0)),
            pl.BlockSpec((1, N_MOD, d), lambda bi, t: (bi, 0, 0)),
            pl.BlockSpec((d, 2 * d_ff), c2),
            pl.BlockSpec((CONV_WIDTH, 2 * d_ff), c2),
            pl.BlockSpec((1, 2 * d_ff), c2),
            pl.BlockSpec((d_ff, d), c2),
            pl.BlockSpec((1, d), c2),
            pl.BlockSpec((1, d), c2),
        ],
        out_specs=pl.BlockSpec((1, tm, d), lambda bi, t: (bi, t, 0)),
        out_shape=jax.ShapeDtypeStruct((b, s, d), F32),
        scratch_shapes=[pltpu.VMEM((tm + HALO, d), BF16), pltpu.VMEM((tm, d), F32)],
        compiler_params=_params("parallel", "parallel"),
        name="conv_ffn",
    )(x3, x3, mod, w_up, conv_w, conv_b, w_down, gain, bias)


def kernel(x, c, w_in, b_f, rel_bias, w_br_fox, w_br_chunk, w_out, w_up, conv_w, conv_b, w_down,
           w_ada, b_ada, ln1_g, ln1_b, ln2_g, ln2_b):
    b, s, d = x.shape
    depth = w_in.shape[0]
    alpha = (2.0 * depth) ** 0.25
    rows = b * s
    mod_all = _ada_mod(c, w_ada, b_ada).reshape(depth, b, N_MOD, d)
    bias_tables = _chunk_bias_tables(rel_bias, TQ_CHUNK)

    f0 = 3 * WIDTH
    f1 = f0 + N_HEADS
    xr = x.reshape(rows, d)
    for l in range(depth):
        mod = mod_all[l]
        q_scale = LOG2E / math.sqrt(HEAD_DIM)
        qc0 = f1
        qc1 = f1 + WIDTH
        w_main = jnp.concatenate([w_in[l, :, :WIDTH] * q_scale, w_in[l, :, WIDTH:f0],
                                  w_in[l, :, qc0:qc1] * q_scale, w_in[l, :, qc1:]], axis=1).astype(BF16)
        w_f = jnp.pad(w_in[l, :, f0:f1], ((0, 0), (0, LANES - N_HEADS))).astype(BF16)
        bf_pad = jnp.pad(b_f[l], (0, LANES - N_HEADS)).reshape(1, LANES)

        proj, f = _ln_proj(xr, mod, w_main, w_f, s)
        proj3 = proj.reshape(b, s, MAIN_COLS)
        cqt, fs, vt, vtc = _attn_prep(f.reshape(b, s, LANES), bf_pad, proj3)
        o_a = _fox(proj3, fs, vt, cqt)
        o_c = _chunk_attn(proj3, vtc, bias_tables, l)
        xr = _mix(o_a.reshape(rows, WIDTH), o_c.reshape(rows, WIDTH), proj, xr, mod,
                  w_br_fox[l].astype(BF16), w_br_chunk[l].astype(BF16), w_out[l].astype(BF16),
                  ln1_g[l].reshape(1, d), ln1_b[l].reshape(1, d), s, alpha)

        x3 = _ffn(xr.reshape(b, s, d), mod, w_up[l].astype(BF16), conv_w[l], conv_b[l].reshape(1, -1),
                  w_down[l].astype(BF16), ln2_g[l].reshape(1, d), ln2_b[l].reshape(1, d), alpha)
        xr = x3.reshape(rows, d)
    return xr.reshape(b, s, d)
```

```python
import functools
import math

import numpy as np
import jax
import jax.numpy as jnp
from jax import lax
from jax.experimental import pallas as pl
from jax.experimental.pallas import tpu as pltpu

F32 = jnp.float32
BF16 = jnp.bfloat16

D_MODEL = 1024
HEAD_DIM = 64
N_HEADS = 8
WIDTH = N_HEADS * HEAD_DIM
CHUNK = 64
LEFT_CHUNKS = 8
REL_CLIP = 128
CONV_WIDTH = 3
LN_EPS = 1e-5
N_MOD = 6
LANES = 128
PAIR = 2 * HEAD_DIM
N_PAIRS = N_HEADS // 2
MAIN_COLS = 3 * WIDTH + 3 * WIDTH + 2 * D_MODEL
LOG2E = math.log2(math.e)
NEG = -1e30
VMEM_LIMIT = 56 * 1024 * 1024

TM_PROJ = 1024
TN_PROJ = 5120
T_CUM = 512
TQ_FOX = 1024
TK_FOX = 1024
TQ_CHUNK = 256
NSUB_CHUNK = 16
TM_MIX = 1024
TM_FFN = 512
CK_FFN = 2816
VT_ROWS = HEAD_DIM + 16
HALO = 16


def _ln(x):
    mu = jnp.mean(x, axis=-1, keepdims=True)
    xc = x - mu
    var = jnp.mean(xc * xc, axis=-1, keepdims=True)
    return xc * lax.rsqrt(var + LN_EPS)


def _dot(a, b):
    return jnp.dot(a, b, preferred_element_type=F32)


def _split3(v):
    hi = v.astype(BF16)
    r = v - hi.astype(F32)
    mid = r.astype(BF16)
    lo = (r - mid.astype(F32)).astype(BF16)
    return hi, mid, lo


def _params(*sem):
    return pltpu.CompilerParams(dimension_semantics=sem, vmem_limit_bytes=VMEM_LIMIT)


def _mod_kernel(c_ref, w_ref, b_ref, o_ref):
    c = c_ref[...]
    cond = c * jax.nn.sigmoid(c)
    a_hi, a_mid, _ = _split3(cond)
    w = w_ref[0]
    w_hi = w.astype(BF16)
    w_lo = (w - w_hi.astype(F32)).astype(BF16)
    acc = _dot(a_hi, w_hi) + _dot(a_mid, w_hi) + _dot(a_hi, w_lo)
    o_ref[0] = acc + b_ref[0]


def _ada_mod(c, w_ada, b_ada):
    depth, d, n = w_ada.shape
    b = c.shape[0]
    rows = 8
    c_pad = jnp.pad(c, ((0, rows - b), (0, 0)))
    tn = 1024
    out = pl.pallas_call(
        _mod_kernel,
        grid=(depth, n // tn),
        in_specs=[
            pl.BlockSpec((rows, d), lambda l, j: (0, 0)),
            pl.BlockSpec((1, d, tn), lambda l, j: (l, 0, j)),
            pl.BlockSpec((1, 1, tn), lambda l, j: (l, 0, j)),
        ],
        out_specs=pl.BlockSpec((1, rows, tn), lambda l, j: (l, 0, j)),
        out_shape=jax.ShapeDtypeStruct((depth, rows, n), F32),
        compiler_params=_params("parallel", "parallel"),
        name="ada_mod",
    )(c_pad, w_ada, b_ada.reshape(depth, 1, n))
    return out[:, :b, :]


def _ln_proj_kernel(x_ref, mod_ref, w_ref, wf_ref, o_ref, f_ref, h_ref):
    @pl.when(pl.program_id(1) == 0)
    def _():
        y = _ln(x_ref[...])
        h = (y * (1.0 + mod_ref[0, 1:2, :]) + mod_ref[0, 0:1, :]).astype(BF16)
        h_ref[...] = h
        f_ref[...] = _dot(h, wf_ref[...])

    o_ref[...] = _dot(h_ref[...], w_ref[...]).astype(BF16)


def _ln_proj(xr, mod, w_main, w_f, seq):
    rows, d = xr.shape
    tm, tn = TM_PROJ, TN_PROJ
    n = w_main.shape[1]
    tiles_per_seq = seq // tm
    return pl.pallas_call(
        _ln_proj_kernel,
        grid=(rows // tm, n // tn),
        in_specs=[
            pl.BlockSpec((tm, d), lambda i, j: (i, 0)),
            pl.BlockSpec((1, N_MOD, d), lambda i, j: (i // tiles_per_seq, 0, 0)),
            pl.BlockSpec((d, tn), lambda i, j: (0, j),
                         pipeline_mode=pl.Buffered(1) if n == tn else None),
            pl.BlockSpec((d, LANES), lambda i, j: (0, 0), pipeline_mode=pl.Buffered(1)),
        ],
        out_specs=[
            pl.BlockSpec((tm, tn), lambda i, j: (i, j)),
            pl.BlockSpec((tm, LANES), lambda i, j: (i, 0)),
        ],
        out_shape=[
            jax.ShapeDtypeStruct((rows, n), BF16),
            jax.ShapeDtypeStruct((rows, LANES), F32),
        ],
        scratch_shapes=[pltpu.VMEM((tm, d), BF16)],
        compiler_params=_params("parallel", "arbitrary"),
        name="ln_proj",
    )(xr, mod, w_main, w_f)


def _attn_prep_kernel(f_ref, bf_ref, v_ref, vc_ref, cqt_ref, fs_ref, vt_ref, vtc_ref, carry_ref):
    @pl.when(pl.program_id(1) == 0)
    def _():
        carry_ref[...] = jnp.zeros_like(carry_ref)

    z = f_ref[0] + bf_ref[...]
    lf = jnp.minimum(z, 0.0) - jnp.log(1.0 + jnp.exp(-jnp.abs(z)))
    hi, mid, lo = _split3(lf)
    tc = lf.shape[0]
    row = lax.broadcasted_iota(jnp.int32, (tc, tc), 0)
    col = lax.broadcasted_iota(jnp.int32, (tc, tc), 1)
    tri = jnp.where(col <= row, 1.0, 0.0).astype(BF16)
    cum = _dot(tri, hi) + _dot(tri, mid) + _dot(tri, lo) + carry_ref[...]
    carry_ref[...] = cum[tc - 1:tc, :]

    cum2 = cum * LOG2E
    cqt_ref[0] = cum2.T[0:N_HEADS, :]
    neg = -cum2
    n_hi = neg.astype(BF16).astype(F32)
    r1 = neg - n_hi
    n_mid = r1.astype(BF16).astype(F32)
    n_lo = r1 - n_mid
    lane = lax.broadcasted_iota(jnp.int32, (1, LANES), 1)
    placed = jnp.where(lane < N_HEADS, n_hi,
                       jnp.where(lane < 2 * N_HEADS, pltpu.roll(n_mid, N_HEADS, axis=1),
                                 jnp.where(lane < 3 * N_HEADS, pltpu.roll(n_lo, 2 * N_HEADS, axis=1), 0.0)))
    fs_ref[0] = placed.astype(BF16)

    extra = VT_ROWS - HEAD_DIM
    ones_row = jnp.where(lax.broadcasted_iota(jnp.int32, (extra, tc), 0) == 0, 1.0, 0.0)
    for src_ref, dst_ref in ((v_ref, vt_ref), (vc_ref, vtc_ref)):
        for p in range(N_PAIRS):
            vt = src_ref[0, :, p * PAIR:(p + 1) * PAIR].astype(F32).T
            dst_ref[0, 2 * p] = jnp.concatenate([vt[0:HEAD_DIM], ones_row], axis=0).astype(BF16)
            dst_ref[0, 2 * p + 1] = jnp.concatenate([vt[HEAD_DIM:], ones_row], axis=0).astype(BF16)


def _attn_prep(f3, bf_pad, proj3):
    b, s, _ = f3.shape
    tc = T_CUM
    v_blk = 2
    vc_blk = 5
    return pl.pallas_call(
        _attn_prep_kernel,
        grid=(b, s // tc),
        in_specs=[
            pl.BlockSpec((1, tc, LANES), lambda bi, t: (bi, t, 0)),
            pl.BlockSpec((1, LANES), lambda bi, t: (0, 0)),
            pl.BlockSpec((1, tc, WIDTH), lambda bi, t: (bi, t, v_blk)),
            pl.BlockSpec((1, tc, WIDTH), lambda bi, t: (bi, t, vc_blk)),
        ],
        out_specs=[
            pl.BlockSpec((1, N_HEADS, tc), lambda bi, t: (bi, 0, t)),
            pl.BlockSpec((1, tc, LANES), lambda bi, t: (bi, t, 0)),
            pl.BlockSpec((1, N_HEADS, VT_ROWS, tc), lambda bi, t: (bi, 0, 0, t)),
            pl.BlockSpec((1, N_HEADS, VT_ROWS, tc), lambda bi, t: (bi, 0, 0, t)),
        ],
        out_shape=[
            jax.ShapeDtypeStruct((b, N_HEADS, s), F32),
            jax.ShapeDtypeStruct((b, s, LANES), BF16),
            jax.ShapeDtypeStruct((b, N_HEADS, VT_ROWS, s), BF16),
            jax.ShapeDtypeStruct((b, N_HEADS, VT_ROWS, s), BF16),
        ],
        scratch_shapes=[pltpu.VMEM((1, LANES), F32)],
        compiler_params=_params("parallel", "arbitrary"),
        name="attn_prep",
    )(f3, bf_pad, proj3, proj3)


def _fox_kernel(q_ref, k_ref, fs_ref, vt_ref, cqt_ref, o_ref, qt_ref, sta_ref, stb_ref, cma_ref, cmb_ref,
                m_ref, acc_ref, *, tq, tk):
    p = pl.program_id(1)
    i = pl.program_id(2)
    q0 = i * tq
    n_full = q0 // tk
    qt = q_ref[0].astype(F32).T
    row = lax.broadcasted_iota(jnp.int32, (PAIR, 1), 0)
    qts, cqs = [], []
    for hh in range(2):
        h = 2 * p + hh
        in_head = (row < HEAD_DIM) if hh == 0 else (row >= HEAD_DIM)
        q_rows = jnp.where(in_head, qt, 0.0)
        pick = (row == h) | (row == h + N_HEADS) | (row == h + 2 * N_HEADS)
        one_rows = jnp.broadcast_to(jnp.where(pick, 1.0, 0.0), (PAIR, tq))
        qts.append(jnp.concatenate([q_rows, one_rows], axis=0).astype(BF16))
        cqs.append(cqt_ref[0, pl.ds(h, 1), :])
    qt_both = jnp.concatenate(qts, axis=1)
    cq = jnp.concatenate(cqs, axis=1)

    qt_ref[...] = qt_both
    m_ref[...] = jnp.full((1, 2 * tq), NEG, F32)
    acc_ref[...] = jnp.zeros((2, VT_ROWS, tq), F32)

    def qk(j, st_ref, cm_ref):
        ks = pl.multiple_of(j * tk, tk)
        kf = jnp.concatenate([k_ref[0, pl.ds(ks, tk), :], fs_ref[0, pl.ds(ks, tk), :]], axis=1)
        st = _dot(kf, qt_ref[...])
        st_ref[...] = st
        cm_ref[...] = jnp.max(st, axis=0, keepdims=True)

    def soft_pv(j, st_ref, cm_ref):
        ks = pl.multiple_of(j * tk, tk)
        m = m_ref[...]
        m_new = jnp.maximum(m, cm_ref[...] + cq)
        alpha = jnp.exp2(m - m_new)
        pb = jnp.exp2(st_ref[...] - (m_new - cq)).astype(BF16)
        m_ref[...] = m_new
        acc_ref[0] = alpha[:, :tq] * acc_ref[0] + _dot(vt_ref[0, 0, :, pl.ds(ks, tk)], pb[:, :tq])
        acc_ref[1] = alpha[:, tq:] * acc_ref[1] + _dot(vt_ref[0, 1, :, pl.ds(ks, tk)], pb[:, tq:])

    half = tq // 2
    ks_diag = pl.multiple_of(n_full * tk, tk)

    def qk_diag(st_ref):
        kf = jnp.concatenate([k_ref[0, pl.ds(ks_diag, tk), :], fs_ref[0, pl.ds(ks_diag, tk), :]], axis=1)
        qt_all = qt_ref[...]
        st_ref[0:half, :] = _dot(kf[0:half], qt_all)
        qt_hi = jnp.concatenate([qt_all[:, half:tq], qt_all[:, tq + half:]], axis=1)
        hi = _dot(kf[half:], qt_hi)
        st_ref[half:, half:tq] = hi[:, 0:half]
        st_ref[half:, tq + half:] = hi[:, half:]

    def soft_pv_diag(st_ref):
        causal = (lax.broadcasted_iota(jnp.int32, (half, half), 0)
                  <= lax.broadcasted_iota(jnp.int32, (half, half), 1))
        for hh in range(2):
            for c in range(2):
                cols = slice(hh * tq + c * half, hh * tq + (c + 1) * half)
                diag = jnp.where(causal, st_ref[c * half:(c + 1) * half, cols], NEG)
                parts = [diag] if c == 0 else [st_ref[0:half, cols], diag]
                cm = functools.reduce(jnp.maximum, [jnp.max(part, axis=0, keepdims=True) for part in parts])
                m = m_ref[:, cols]
                cq_c = cq[:, cols]
                m_new = jnp.maximum(m, cm + cq_c)
                alpha = jnp.exp2(m - m_new)
                shift = m_new - cq_c
                pb = jnp.concatenate([jnp.exp2(part - shift) for part in parts], axis=0).astype(BF16)
                m_ref[:, cols] = m_new
                out_cols = slice(c * half, (c + 1) * half)
                acc_ref[hh, :, out_cols] = (alpha * acc_ref[hh, :, out_cols]
                                            + _dot(vt_ref[0, hh, :, pl.ds(ks_diag, (c + 1) * half)], pb))

    qk(0, sta_ref, cma_ref)

    def pair(jj, c):
        qk(2 * jj + 1, stb_ref, cmb_ref)
        soft_pv(2 * jj, sta_ref, cma_ref)
        qk(2 * jj + 2, sta_ref, cma_ref)
        soft_pv(2 * jj + 1, stb_ref, cmb_ref)
        return c

    lax.fori_loop(0, n_full // 2, pair, 0)

    @pl.when(n_full % 2 == 1)
    def _():
        qk_diag(stb_ref)
        soft_pv(n_full - 1, sta_ref, cma_ref)
        soft_pv_diag(stb_ref)

    @pl.when(n_full % 2 == 0)
    def _():
        soft_pv_diag(sta_ref)

    ot = jnp.concatenate([acc_ref[hh, 0:HEAD_DIM, :] * (1.0 / acc_ref[hh, HEAD_DIM:HEAD_DIM + 1, :])
                          for hh in range(2)], axis=0)
    o_ref[0] = ot.T.astype(BF16)


def _fox(proj3, fs, vt, cqt):
    b, s, _ = proj3.shape
    tq, tk = TQ_FOX, TK_FOX
    assert tk == tq and s % tk == 0
    return pl.pallas_call(
        functools.partial(_fox_kernel, tq=tq, tk=tk),
        grid=(b, N_PAIRS, s // tq),
        in_specs=[
            pl.BlockSpec((1, tq, PAIR), lambda bi, p, i: (bi, i, p)),
            pl.BlockSpec((1, s, PAIR), lambda bi, p, i: (bi, 0, N_PAIRS + p)),
            pl.BlockSpec((1, s, LANES), lambda bi, p, i: (bi, 0, 0)),
            pl.BlockSpec((1, 2, VT_ROWS, s), lambda bi, p, i: (bi, p, 0, 0)),
            pl.BlockSpec((1, N_HEADS, tq), lambda bi, p, i: (bi, 0, i)),
        ],
        out_specs=pl.BlockSpec((1, tq, PAIR), lambda bi, p, i: (bi, i, p)),
        out_shape=jax.ShapeDtypeStruct((b, s, WIDTH), BF16),
        scratch_shapes=[pltpu.VMEM((2 * PAIR, 2 * tq), BF16), pltpu.VMEM((tk, 2 * tq), F32),
                        pltpu.VMEM((tk, 2 * tq), F32), pltpu.VMEM((1, 2 * tq), F32),
                        pltpu.VMEM((1, 2 * tq), F32), pltpu.VMEM((1, 2 * tq), F32),
                        pltpu.VMEM((2, VT_ROWS, tq), F32)],
        compiler_params=_params("parallel", "parallel", "arbitrary"),
        name="fox_attn",
    )(proj3, proj3, fs, vt, cqt)


def _chunk_kernel(q_ref, k_ref, vt_ref, bias_ref, o_ref, *, tq, nblk, nsub):
    i = pl.program_id(2)
    w = nblk * tq
    qt = q_ref[0].astype(F32).T
    row = lax.broadcasted_iota(jnp.int32, (PAIR, 1), 0)
    head_rows = [row < HEAD_DIM, row >= HEAD_DIM]

    def scores(sb):
        g = nsub * i + sb
        ws = pl.multiple_of(jnp.maximum(g - (nblk - 1), 0) * tq, tq)
        qs = qt[:, sb * tq:(sb + 1) * tq]
        qt_both = jnp.concatenate([jnp.where(hr, qs, 0.0) for hr in head_rows], axis=1).astype(BF16)
        st = _dot(k_ref[0, pl.ds(ws, w), :], qt_both)
        if sb < nblk - 1:
            off = pl.multiple_of(jnp.maximum(nblk - 1 - g, 0) * tq, tq)
            bias = [bias_ref[0, hh, pl.ds(off, w), :] for hh in range(2)]
        else:
            bias = [bias_ref[0, hh, 0:w, :] for hh in range(2)]
        return ws, [st[:, hh * tq:(hh + 1) * tq] + bias[hh] for hh in range(2)]

    def finish(sb, ws, sts):
        cols = []
        for hh in range(2):
            st = sts[hh]
            pb = jnp.exp2(st - jnp.max(st, axis=0, keepdims=True)).astype(BF16)
            acc = _dot(vt_ref[0, hh, :, pl.ds(ws, w)], pb)
            cols.append(acc[0:HEAD_DIM] * (1.0 / acc[HEAD_DIM:HEAD_DIM + 1]))
        o_ref[0, sb * tq:(sb + 1) * tq, :] = jnp.concatenate(cols, axis=0).T.astype(BF16)

    pending = scores(0)
    for sb in range(nsub):
        nxt = scores(sb + 1) if sb + 1 < nsub else None
        finish(sb, *pending)
        pending = nxt


def _chunk_bias_tables(rel_bias, tq):
    left = LEFT_CHUNKS * CHUNK
    w = tq + left
    row_len = 1 << (tq + w - 2).bit_length()
    period = row_len + 1
    lead = rel_bias.shape[:-1]
    top = rel_bias[..., 2 * REL_CLIP:]
    bot = rel_bias[..., :1]
    n_top = left - REL_CLIP + 1
    n_bot = w - n_top - 2 * REL_CLIP
    u = jnp.concatenate([
        jnp.broadcast_to(top, lead + (n_top,)),
        rel_bias[..., :2 * REL_CLIP][..., ::-1],
        jnp.broadcast_to(bot, lead + (n_bot + period - w - (tq - 1),)),
        jnp.broadcast_to(top, lead + (tq - 1,)),
    ], axis=-1).astype(F32)
    flat = jnp.tile(u, (1,) * len(lead) + (tq + 1,))[..., :tq * row_len]
    bias = flat.reshape(lead + (tq, row_len))[..., :w]
    qc = np.arange(tq)[:, None] // CHUNK
    kc = np.arange(w)[None, :] // CHUNK
    band = (kc >= qc) & (kc <= qc + LEFT_CHUNKS)
    full = jnp.swapaxes(jnp.where(jnp.asarray(band), bias * LOG2E, NEG), -1, -2)
    gone = jnp.full(lead + (left, tq), NEG, F32)
    return jnp.concatenate([full, gone], axis=-2)


def _chunk_attn(proj3, vtc, bias_tables, layer):
    b, s, _ = proj3.shape
    tq = TQ_CHUNK
    left = LEFT_CHUNKS * CHUNK
    assert left == 2 * tq and s >= 3 * tq
    nblk = 3
    nsub = NSUB_CHUNK
    assert nsub >= 2 and s % (nsub * tq) == 0
    col0 = 3 * N_PAIRS
    return pl.pallas_call(
        functools.partial(_chunk_kernel, tq=tq, nblk=nblk, nsub=nsub),
        grid=(b, N_PAIRS, s // (nsub * tq)),
        in_specs=[
            pl.BlockSpec((1, nsub * tq, PAIR), lambda bi, p, i: (bi, i, col0 + p)),
            pl.BlockSpec((1, s, PAIR), lambda bi, p, i: (bi, 0, col0 + N_PAIRS + p)),
            pl.BlockSpec((1, 2, VT_ROWS, s), lambda bi, p, i: (bi, p, 0, 0)),
            pl.BlockSpec((1, 2, (2 * nblk - 1) * tq, tq), lambda bi, p, i: (layer, p, 0, 0)),
        ],
        out_specs=pl.BlockSpec((1, nsub * tq, PAIR), lambda bi, p, i: (bi, i, p)),
        out_shape=jax.ShapeDtypeStruct((b, s, WIDTH), BF16),
        compiler_params=_params("parallel", "parallel", "arbitrary"),
        name="chunk_attn",
    )(proj3, proj3, vtc, bias_tables)


def _mix_kernel(oa_ref, oc_ref, ga_ref, gc_ref, x_ref, mod_ref, wa_ref, wc_ref, wo_ref,
                g_ref, b_ref, o_ref, *, alpha):
    ya = _dot(oa_ref[...], wa_ref[...])
    yc = _dot(oc_ref[...], wc_ref[...])
    merged = (jax.nn.sigmoid(ga_ref[...].astype(F32)) * ya
              + jax.nn.sigmoid(gc_ref[...].astype(F32)) * yc)
    mix = _dot(merged.astype(BF16), wo_ref[...])
    z = alpha * x_ref[...] + (1.0 + mod_ref[0, 2:3, :]) * mix
    o_ref[...] = _ln(z) * g_ref[...] + b_ref[...]


def _mix(oa, oc, proj, xr, mod, w_a, w_c, w_o, gain, bias, seq, alpha):
    rows, d = xr.shape
    tm = TM_MIX
    tiles_per_seq = seq // tm
    gate_blk = (2 * 3 * WIDTH) // d
    const = lambda i: (0, 0)
    return pl.pallas_call(
        functools.partial(_mix_kernel, alpha=alpha),
        grid=(rows // tm,),
        in_specs=[
            pl.BlockSpec((tm, WIDTH), lambda i: (i, 0)),
            pl.BlockSpec((tm, WIDTH), lambda i: (i, 0)),
            pl.BlockSpec((tm, d), lambda i: (i, gate_blk)),
            pl.BlockSpec((tm, d), lambda i: (i, gate_blk + 1)),
            pl.BlockSpec((tm, d), lambda i: (i, 0)),
            pl.BlockSpec((1, N_MOD, d), lambda i: (i // tiles_per_seq, 0, 0)),
            pl.BlockSpec((WIDTH, d), const),
            pl.BlockSpec((WIDTH, d), const),
            pl.BlockSpec((d, d), const),
            pl.BlockSpec((1, d), const),
            pl.BlockSpec((1, d), const),
        ],
        out_specs=pl.BlockSpec((tm, d), lambda i: (i, 0)),
        out_shape=jax.ShapeDtypeStruct((rows, d), F32),
        compiler_params=_params("parallel"),
        name="mix_out",
    )(oa, oc, proj, proj, xr, mod, w_a, w_c, w_o, gain, bias)


def _ffn_kernel(x_ref, xh_ref, mod_ref, wu_ref, cw_ref, cb_ref, wd_ref, g_ref, b_ref, o_ref, h_ref, acc_ref,
                *, alpha, tm, ck):
    t = pl.program_id(1)
    d_ff = wd_ref.shape[0]
    x = x_ref[0]
    sc = 1.0 + mod_ref[0, 4:5, :]
    sh = mod_ref[0, 3:4, :]
    h_ref[HALO:, :] = (_ln(x) * sc + sh).astype(BF16)
    keep = jnp.where(t > 0, 1.0, 0.0)
    h_ref[0:HALO, :] = ((_ln(xh_ref[0]) * sc + sh) * keep).astype(BF16)
    acc_ref[...] = jnp.zeros_like(acc_ref)

    def conv(u, lo_col):
        cols = slice(lo_col, lo_col + ck)
        lo = HALO - (CONV_WIDTH - 1)
        y = cb_ref[:, cols]
        for j in range(CONV_WIDTH):
            y = y + cw_ref[j:j + 1, cols] * u[lo + j:lo + j + tm, :]
        return y

    def up(c):
        he = h_ref[...]
        return (_dot(he, wu_ref[:, c * ck:(c + 1) * ck]),
                _dot(he, wu_ref[:, d_ff + c * ck:d_ff + (c + 1) * ck]))

    u = up(0)
    for c in range(d_ff // ck):
        u_next = up(c + 1) if (c + 1) * ck < d_ff else None
        a = conv(u[0], c * ck)
        v = conv(u[1], d_ff + c * ck)
        act = (a * jax.nn.sigmoid(a) * v).astype(BF16)
        acc_ref[...] += _dot(act, wd_ref[c * ck:(c + 1) * ck, :])
        u = u_next
    z = alpha * x + (1.0 + mod_ref[0, 5:6, :]) * acc_ref[...]
    o_ref[0] = _ln(z) * g_ref[...] + b_ref[...]


def _ffn(x3, mod, w_up, conv_w, conv_b, w_down, gain, bias, alpha):
    b, s, d = x3.shape
    tm = TM_FFN
    d_ff = w_down.shape[0]
    assert d_ff % CK_FFN == 0
    halo_per_tile = tm // HALO
    c2 = lambda bi, t: (0, 0)
    return pl.pallas_call(
        functools.partial(_ffn_kernel, alpha=alpha, tm=tm, ck=CK_FFN),
        grid=(b, s // tm),
        in_specs=[
            pl.BlockSpec((1, tm, d), lambda bi, t: (bi, t, 0)),
            pl.BlockSpec((1, HALO, d), lambda bi, t: (bi, jnp.maximum(t * halo_per_tile - 1, 0), 0)),
            pl.BlockSpec((1, N_MOD, d), lambda bi, t: (bi, 0, 0)),
            pl.BlockSpec((d, 2 * d_ff), c2),
            pl.BlockSpec((CONV_WIDTH, 2 * d_ff), c2),
            pl.BlockSpec((1, 2 * d_ff), c2),
            pl.BlockSpec((d_ff, d), c2),
            pl.BlockSpec((1, d), c2),
            pl.BlockSpec((1, d), c2),
        ],
        out_specs=pl.BlockSpec((1, tm, d), lambda bi, t: (bi, t, 0)),
        out_shape=jax.ShapeDtypeStruct((b, s, d), F32),
        scratch_shapes=[pltpu.VMEM((tm + HALO, d), BF16), pltpu.VMEM((tm, d), F32)],
        compiler_params=_params("parallel", "parallel"),
        name="conv_ffn",
    )(x3, x3, mod, w_up, conv_w, conv_b, w_down, gain, bias)


def kernel(x, c, w_in, b_f, rel_bias, w_br_fox, w_br_chunk, w_out, w_up, conv_w, conv_b, w_down,
           w_ada, b_ada, ln1_g, ln1_b, ln2_g, ln2_b):
    b, s, d = x.shape
    depth = w_in.shape[0]
    alpha = (2.0 * depth) ** 0.25
    rows = b * s
    mod_all = _ada_mod(c, w_ada, b_ada).reshape(depth, b, N_MOD, d)
    bias_tables = _chunk_bias_tables(rel_bias, TQ_CHUNK)

    f0 = 3 * WIDTH
    f1 = f0 + N_HEADS
    xr = x.reshape(rows, d)
    for l in range(depth):
        mod = mod_all[l]
        q_scale = LOG2E / math.sqrt(HEAD_DIM)
        qc0 = f1
        qc1 = f1 + WIDTH
        w_main = jnp.concatenate([w_in[l, :, :WIDTH] * q_scale, w_in[l, :, WIDTH:f0],
                                  w_in[l, :, qc0:qc1] * q_scale, w_in[l, :, qc1:]], axis=1).astype(BF16)
        w_f = jnp.pad(w_in[l, :, f0:f1], ((0, 0), (0, LANES - N_HEADS))).astype(BF16)
        bf_pad = jnp.pad(b_f[l], (0, LANES - N_HEADS)).reshape(1, LANES)

        proj, f = _ln_proj(xr, mod, w_main, w_f, s)
        proj3 = proj.reshape(b, s, MAIN_COLS)
        cqt, fs, vt, vtc = _attn_prep(f.reshape(b, s, LANES), bf_pad, proj3)
        o_a = _fox(proj3, fs, vt, cqt)
        o_c = _chunk_attn(proj3, vtc, bias_tables, l)
        xr = _mix(o_a.reshape(rows, WIDTH), o_c.reshape(rows, WIDTH), proj, xr, mod,
                  w_br_fox[l].astype(BF16), w_br_chunk[l].astype(BF16), w_out[l].astype(BF16),
                  ln1_g[l].reshape(1, d), ln1_b[l].reshape(1, d), s, alpha)

        x3 = _ffn(xr.reshape(b, s, d), mod, w_up[l].astype(BF16), conv_w[l], conv_b[l].reshape(1, -1),
                  w_down[l].astype(BF16), ln2_g[l].reshape(1, d), ln2_b[l].reshape(1, d), alpha)
        xr = x3.reshape(rows, d)
    return xr.reshape(b, s, d)
```

```python
import functools
import math

import numpy as np
import jax
import jax.numpy as jnp
from jax import lax
from jax.experimental import pallas as pl
from jax.experimental.pallas import tpu as pltpu

F32 = jnp.float32
BF16 = jnp.bfloat16

D_MODEL = 1024
HEAD_DIM = 64
N_HEADS = 8
WIDTH = N_HEADS * HEAD_DIM
CHUNK = 64
LEFT_CHUNKS = 8
REL_CLIP = 128
CONV_WIDTH = 3
LN_EPS = 1e-5
N_MOD = 6
LANES = 128
PAIR = 2 * HEAD_DIM
N_PAIRS = N_HEADS // 2
MAIN_COLS = 3 * WIDTH + 3 * WIDTH + 2 * D_MODEL
LOG2E = math.log2(math.e)
NEG = -1e30
VMEM_LIMIT = 56 * 1024 * 1024

TM_PROJ = 1024
TN_PROJ = 5120
T_CUM = 512
TQ_FOX = 1024
TK_FOX = 1024
TQ_CHUNK = 256
NSUB_CHUNK = 32
TM_MIX = 1024
TM_FFN = 512
CK_FFN = 2816
VT_ROWS = HEAD_DIM + 16
HALO = 16


def _ln(x):
    mu = jnp.mean(x, axis=-1, keepdims=True)
    xc = x - mu
    var = jnp.mean(xc * xc, axis=-1, keepdims=True)
    return xc * lax.rsqrt(var + LN_EPS)


def _dot(a, b):
    return jnp.dot(a, b, preferred_element_type=F32)


def _split3(v):
    hi = v.astype(BF16)
    r = v - hi.astype(F32)
    mid = r.astype(BF16)
    lo = (r - mid.astype(F32)).astype(BF16)
    return hi, mid, lo


def _params(*sem):
    return pltpu.CompilerParams(dimension_semantics=sem, vmem_limit_bytes=VMEM_LIMIT)


def _mod_kernel(c_ref, w_ref, b_ref, o_ref):
    c = c_ref[...]
    cond = c * jax.nn.sigmoid(c)
    a_hi, a_mid, _ = _split3(cond)
    w = w_ref[0]
    w_hi = w.astype(BF16)
    w_lo = (w - w_hi.astype(F32)).astype(BF16)
    acc = _dot(a_hi, w_hi) + _dot(a_mid, w_hi) + _dot(a_hi, w_lo)
    o_ref[0] = acc + b_ref[0]


def _ada_mod(c, w_ada, b_ada):
    depth, d, n = w_ada.shape
    b = c.shape[0]
    rows = 8
    c_pad = jnp.pad(c, ((0, rows - b), (0, 0)))
    tn = 1024
    out = pl.pallas_call(
        _mod_kernel,
        grid=(depth, n // tn),
        in_specs=[
            pl.BlockSpec((rows, d), lambda l, j: (0, 0)),
            pl.BlockSpec((1, d, tn), lambda l, j: (l, 0, j)),
            pl.BlockSpec((1, 1, tn), lambda l, j: (l, 0, j)),
        ],
        out_specs=pl.BlockSpec((1, rows, tn), lambda l, j: (l, 0, j)),
        out_shape=jax.ShapeDtypeStruct((depth, rows, n), F32),
        compiler_params=_params("parallel", "parallel"),
        name="ada_mod",
    )(c_pad, w_ada, b_ada.reshape(depth, 1, n))
    return out[:, :b, :]


def _ln_proj_kernel(x_ref, mod_ref, w_ref, wf_ref, o_ref, f_ref, h_ref):
    @pl.when(pl.program_id(1) == 0)
    def _():
        y = _ln(x_ref[...])
        h = (y * (1.0 + mod_ref[0, 1:2, :]) + mod_ref[0, 0:1, :]).astype(BF16)
        h_ref[...] = h
        f_ref[...] = _dot(h, wf_ref[...])

    o_ref[...] = _dot(h_ref[...], w_ref[...]).astype(BF16)


def _ln_proj(xr, mod, w_main, w_f, seq):
    rows, d = xr.shape
    tm, tn = TM_PROJ, TN_PROJ
    n = w_main.shape[1]
    tiles_per_seq = seq // tm
    return pl.pallas_call(
        _ln_proj_kernel,
        grid=(rows // tm, n // tn),
        in_specs=[
            pl.BlockSpec((tm, d), lambda i, j: (i, 0)),
            pl.BlockSpec((1, N_MOD, d), lambda i, j: (i // tiles_per_seq, 0, 0)),
            pl.BlockSpec((d, tn), lambda i, j: (0, j),
                         pipeline_mode=pl.Buffered(1) if n == tn else None),
            pl.BlockSpec((d, LANES), lambda i, j: (0, 0), pipeline_mode=pl.Buffered(1)),
        ],
        out_specs=[
            pl.BlockSpec((tm, tn), lambda i, j: (i, j)),
            pl.BlockSpec((tm, LANES), lambda i, j: (i, 0)),
        ],
        out_shape=[
            jax.ShapeDtypeStruct((rows, n), BF16),
            jax.ShapeDtypeStruct((rows, LANES), F32),
        ],
        scratch_shapes=[pltpu.VMEM((tm, d), BF16)],
        compiler_params=_params("parallel", "arbitrary"),
        name="ln_proj",
    )(xr, mod, w_main, w_f)


def _attn_prep_kernel(f_ref, bf_ref, v_ref, vc_ref, cqt_ref, fs_ref, vt_ref, vtc_ref, carry_ref):
    @pl.when(pl.program_id(1) == 0)
    def _():
        carry_ref[...] = jnp.zeros_like(carry_ref)

    z = f_ref[0] + bf_ref[...]
    lf = jnp.minimum(z, 0.0) - jnp.log(1.0 + jnp.exp(-jnp.abs(z)))
    hi, mid, lo = _split3(lf)
    tc = lf.shape[0]
    row = lax.broadcasted_iota(jnp.int32, (tc, tc), 0)
    col = lax.broadcasted_iota(jnp.int32, (tc, tc), 1)
    tri = jnp.where(col <= row, 1.0, 0.0).astype(BF16)
    cum = _dot(tri, hi) + _dot(tri, mid) + _dot(tri, lo) + carry_ref[...]
    carry_ref[...] = cum[tc - 1:tc, :]

    cum2 = cum * LOG2E
    cqt_ref[0] = cum2.T[0:N_HEADS, :]
    neg = -cum2
    n_hi = neg.astype(BF16).astype(F32)
    r1 = neg - n_hi
    n_mid = r1.astype(BF16).astype(F32)
    n_lo = r1 - n_mid
    lane = lax.broadcasted_iota(jnp.int32, (1, LANES), 1)
    placed = jnp.where(lane < N_HEADS, n_hi,
                       jnp.where(lane < 2 * N_HEADS, pltpu.roll(n_mid, N_HEADS, axis=1),
                                 jnp.where(lane < 3 * N_HEADS, pltpu.roll(n_lo, 2 * N_HEADS, axis=1), 0.0)))
    fs_ref[0] = placed.astype(BF16)

    extra = VT_ROWS - HEAD_DIM
    ones_row = jnp.where(lax.broadcasted_iota(jnp.int32, (extra, tc), 0) == 0, 1.0, 0.0)
    for src_ref, dst_ref in ((v_ref, vt_ref), (vc_ref, vtc_ref)):
        for p in range(N_PAIRS):
            vt = src_ref[0, :, p * PAIR:(p + 1) * PAIR].astype(F32).T
            dst_ref[0, 2 * p] = jnp.concatenate([vt[0:HEAD_DIM], ones_row], axis=0).astype(BF16)
            dst_ref[0, 2 * p + 1] = jnp.concatenate([vt[HEAD_DIM:], ones_row], axis=0).astype(BF16)


def _attn_prep(f3, bf_pad, proj3):
    b, s, _ = f3.shape
    tc = T_CUM
    v_blk = 2
    vc_blk = 5
    return pl.pallas_call(
        _attn_prep_kernel,
        grid=(b, s // tc),
        in_specs=[
            pl.BlockSpec((1, tc, LANES), lambda bi, t: (bi, t, 0)),
            pl.BlockSpec((1, LANES), lambda bi, t: (0, 0)),
            pl.BlockSpec((1, tc, WIDTH), lambda bi, t: (bi, t, v_blk)),
            pl.BlockSpec((1, tc, WIDTH), lambda bi, t: (bi, t, vc_blk)),
        ],
        out_specs=[
            pl.BlockSpec((1, N_HEADS, tc), lambda bi, t: (bi, 0, t)),
            pl.BlockSpec((1, tc, LANES), lambda bi, t: (bi, t, 0)),
            pl.BlockSpec((1, N_HEADS, VT_ROWS, tc), lambda bi, t: (bi, 0, 0, t)),
            pl.BlockSpec((1, N_HEADS, VT_ROWS, tc), lambda bi, t: (bi, 0, 0, t)),
        ],
        out_shape=[
            jax.ShapeDtypeStruct((b, N_HEADS, s), F32),
            jax.ShapeDtypeStruct((b, s, LANES), BF16),
            jax.ShapeDtypeStruct((b, N_HEADS, VT_ROWS, s), BF16),
            jax.ShapeDtypeStruct((b, N_HEADS, VT_ROWS, s), BF16),
        ],
        scratch_shapes=[pltpu.VMEM((1, LANES), F32)],
        compiler_params=_params("parallel", "arbitrary"),
        name="attn_prep",
    )(f3, bf_pad, proj3, proj3)


def _fox_kernel(q_ref, k_ref, fs_ref, vt_ref, cqt_ref, o_ref, qt_ref, sta_ref, stb_ref, cma_ref, cmb_ref,
                m_ref, acc_ref, *, tq, tk):
    p = pl.program_id(1)
    i = pl.program_id(2)
    q0 = i * tq
    n_full = q0 // tk
    qt = q_ref[0].astype(F32).T
    row = lax.broadcasted_iota(jnp.int32, (PAIR, 1), 0)
    qts, cqs = [], []
    for hh in range(2):
        h = 2 * p + hh
        in_head = (row < HEAD_DIM) if hh == 0 else (row >= HEAD_DIM)
        q_rows = jnp.where(in_head, qt, 0.0)
        pick = (row == h) | (row == h + N_HEADS) | (row == h + 2 * N_HEADS)
        one_rows = jnp.broadcast_to(jnp.where(pick, 1.0, 0.0), (PAIR, tq))
        qts.append(jnp.concatenate([q_rows, one_rows], axis=0).astype(BF16))
        cqs.append(cqt_ref[0, pl.ds(h, 1), :])
    qt_both = jnp.concatenate(qts, axis=1)
    cq = jnp.concatenate(cqs, axis=1)

    qt_ref[...] = qt_both
    m_ref[...] = jnp.full((1, 2 * tq), NEG, F32)
    acc_ref[...] = jnp.zeros((2, VT_ROWS, tq), F32)

    def qk(j, st_ref, cm_ref):
        ks = pl.multiple_of(j * tk, tk)
        kf = jnp.concatenate([k_ref[0, pl.ds(ks, tk), :], fs_ref[0, pl.ds(ks, tk), :]], axis=1)
        st = _dot(kf, qt_ref[...])
        st_ref[...] = st
        cm_ref[...] = jnp.max(st, axis=0, keepdims=True)

    def soft_pv(j, st_ref, cm_ref):
        ks = pl.multiple_of(j * tk, tk)
        m = m_ref[...]
        m_new = jnp.maximum(m, cm_ref[...] + cq)
        alpha = jnp.exp2(m - m_new)
        pb = jnp.exp2(st_ref[...] - (m_new - cq)).astype(BF16)
        m_ref[...] = m_new
        acc_ref[0] = alpha[:, :tq] * acc_ref[0] + _dot(vt_ref[0, 0, :, pl.ds(ks, tk)], pb[:, :tq])
        acc_ref[1] = alpha[:, tq:] * acc_ref[1] + _dot(vt_ref[0, 1, :, pl.ds(ks, tk)], pb[:, tq:])

    half = tq // 2
    ks_diag = pl.multiple_of(n_full * tk, tk)

    def qk_diag(st_ref):
        kf = jnp.concatenate([k_ref[0, pl.ds(ks_diag, tk), :], fs_ref[0, pl.ds(ks_diag, tk), :]], axis=1)
        qt_all = qt_ref[...]
        st_ref[0:half, :] = _dot(kf[0:half], qt_all)
        qt_hi = jnp.concatenate([qt_all[:, half:tq], qt_all[:, tq + half:]], axis=1)
        hi = _dot(kf[half:], qt_hi)
        st_ref[half:, half:tq] = hi[:, 0:half]
        st_ref[half:, tq + half:] = hi[:, half:]

    def soft_pv_diag(st_ref):
        causal = (lax.broadcasted_iota(jnp.int32, (half, half), 0)
                  <= lax.broadcasted_iota(jnp.int32, (half, half), 1))
        for hh in range(2):
            for c in range(2):
                cols = slice(hh * tq + c * half, hh * tq + (c + 1) * half)
                diag = jnp.where(causal, st_ref[c * half:(c + 1) * half, cols], NEG)
                parts = [diag] if c == 0 else [st_ref[0:half, cols], diag]
                cm = functools.reduce(jnp.maximum, [jnp.max(part, axis=0, keepdims=True) for part in parts])
                m = m_ref[:, cols]
                cq_c = cq[:, cols]
                m_new = jnp.maximum(m, cm + cq_c)
                alpha = jnp.exp2(m - m_new)
                shift = m_new - cq_c
                pb = jnp.concatenate([jnp.exp2(part - shift) for part in parts], axis=0).astype(BF16)
                m_ref[:, cols] = m_new
                out_cols = slice(c * half, (c + 1) * half)
                acc_ref[hh, :, out_cols] = (alpha * acc_ref[hh, :, out_cols]
                                            + _dot(vt_ref[0, hh, :, pl.ds(ks_diag, (c + 1) * half)], pb))

    qk(0, sta_ref, cma_ref)

    def pair(jj, c):
        qk(2 * jj + 1, stb_ref, cmb_ref)
        soft_pv(2 * jj, sta_ref, cma_ref)
        qk(2 * jj + 2, sta_ref, cma_ref)
        soft_pv(2 * jj + 1, stb_ref, cmb_ref)
        return c

    lax.fori_loop(0, n_full // 2, pair, 0)

    @pl.when(n_full % 2 == 1)
    def _():
        qk_diag(stb_ref)
        soft_pv(n_full - 1, sta_ref, cma_ref)
        soft_pv_diag(stb_ref)

    @pl.when(n_full % 2 == 0)
    def _():
        soft_pv_diag(sta_ref)

    ot = jnp.concatenate([acc_ref[hh, 0:HEAD_DIM, :] * (1.0 / acc_ref[hh, HEAD_DIM:HEAD_DIM + 1, :])
                          for hh in range(2)], axis=0)
    o_ref[0] = ot.T.astype(BF16)


def _fox(proj3, fs, vt, cqt):
    b, s, _ = proj3.shape
    tq, tk = TQ_FOX, TK_FOX
    assert tk == tq and s % tk == 0
    return pl.pallas_call(
        functools.partial(_fox_kernel, tq=tq, tk=tk),
        grid=(b, N_PAIRS, s // tq),
        in_specs=[
            pl.BlockSpec((1, tq, PAIR), lambda bi, p, i: (bi, i, p)),
            pl.BlockSpec((1, s, PAIR), lambda bi, p, i: (bi, 0, N_PAIRS + p)),
            pl.BlockSpec((1, s, LANES), lambda bi, p, i: (bi, 0, 0)),
            pl.BlockSpec((1, 2, VT_ROWS, s), lambda bi, p, i: (bi, p, 0, 0)),
            pl.BlockSpec((1, N_HEADS, tq), lambda bi, p, i: (bi, 0, i)),
        ],
        out_specs=pl.BlockSpec((1, tq, PAIR), lambda bi, p, i: (bi, i, p)),
        out_shape=jax.ShapeDtypeStruct((b, s, WIDTH), BF16),
        scratch_shapes=[pltpu.VMEM((2 * PAIR, 2 * tq), BF16), pltpu.VMEM((tk, 2 * tq), F32),
                        pltpu.VMEM((tk, 2 * tq), F32), pltpu.VMEM((1, 2 * tq), F32),
                        pltpu.VMEM((1, 2 * tq), F32), pltpu.VMEM((1, 2 * tq), F32),
                        pltpu.VMEM((2, VT_ROWS, tq), F32)],
        compiler_params=_params("parallel", "parallel", "arbitrary"),
        name="fox_attn",
    )(proj3, proj3, fs, vt, cqt)


def _chunk_kernel(q_ref, k_ref, vt_ref, bias_ref, o_ref, *, tq, nblk, nsub):
    i = pl.program_id(2)
    w = nblk * tq
    qt = q_ref[0].astype(F32).T
    row = lax.broadcasted_iota(jnp.int32, (PAIR, 1), 0)
    head_rows = [row < HEAD_DIM, row >= HEAD_DIM]

    def scores(sb):
        g = nsub * i + sb
        ws = pl.multiple_of(jnp.maximum(g - (nblk - 1), 0) * tq, tq)
        qs = qt[:, sb * tq:(sb + 1) * tq]
        qt_both = jnp.concatenate([jnp.where(hr, qs, 0.0) for hr in head_rows], axis=1).astype(BF16)
        st = _dot(k_ref[0, pl.ds(ws, w), :], qt_both)
        if sb < nblk - 1:
            off = pl.multiple_of(jnp.maximum(nblk - 1 - g, 0) * tq, tq)
            bias = [bias_ref[0, hh, pl.ds(off, w), :] for hh in range(2)]
        else:
            bias = [bias_ref[0, hh, 0:w, :] for hh in range(2)]
        return ws, [st[:, hh * tq:(hh + 1) * tq] + bias[hh] for hh in range(2)]

    def finish(sb, ws, sts):
        cols = []
        for hh in range(2):
            st = sts[hh]
            pb = jnp.exp2(st - jnp.max(st, axis=0, keepdims=True)).astype(BF16)
            acc = _dot(vt_ref[0, hh, :, pl.ds(ws, w)], pb)
            cols.append(acc[0:HEAD_DIM] * (1.0 / acc[HEAD_DIM:HEAD_DIM + 1]))
        o_ref[0, sb * tq:(sb + 1) * tq, :] = jnp.concatenate(cols, axis=0).T.astype(BF16)

    pending = scores(0)
    for sb in range(nsub):
        nxt = scores(sb + 1) if sb + 1 < nsub else None
        finish(sb, *pending)
        pending = nxt


def _chunk_bias_tables(rel_bias, tq):
    left = LEFT_CHUNKS * CHUNK
    w = tq + left
    row_len = 1 << (tq + w - 2).bit_length()
    period = row_len + 1
    lead = rel_bias.shape[:-1]
    top = rel_bias[..., 2 * REL_CLIP:]
    bot = rel_bias[..., :1]
    n_top = left - REL_CLIP + 1
    n_bot = w - n_top - 2 * REL_CLIP
    u = jnp.concatenate([
        jnp.broadcast_to(top, lead + (n_top,)),
        rel_bias[..., :2 * REL_CLIP][..., ::-1],
        jnp.broadcast_to(bot, lead + (n_bot + period - w - (tq - 1),)),
        jnp.broadcast_to(top, lead + (tq - 1,)),
    ], axis=-1).astype(F32)
    flat = jnp.tile(u, (1,) * len(lead) + (tq + 1,))[..., :tq * row_len]
    bias = flat.reshape(lead + (tq, row_len))[..., :w]
    qc = np.arange(tq)[:, None] // CHUNK
    kc = np.arange(w)[None, :] // CHUNK
    band = (kc >= qc) & (kc <= qc + LEFT_CHUNKS)
    full = jnp.swapaxes(jnp.where(jnp.asarray(band), bias * LOG2E, NEG), -1, -2)
    gone = jnp.full(lead + (left, tq), NEG, F32)
    return jnp.concatenate([full, gone], axis=-2)


def _chunk_attn(proj3, vtc, bias_tables, layer):
    b, s, _ = proj3.shape
    tq = TQ_CHUNK
    left = LEFT_CHUNKS * CHUNK
    assert left == 2 * tq and s >= 3 * tq
    nblk = 3
    nsub = NSUB_CHUNK
    assert nsub >= 2 and s % (nsub * tq) == 0
    col0 = 3 * N_PAIRS
    return pl.pallas_call(
        functools.partial(_chunk_kernel, tq=tq, nblk=nblk, nsub=nsub),
        grid=(b, N_PAIRS, s // (nsub * tq)),
        in_specs=[
            pl.BlockSpec((1, nsub * tq, PAIR), lambda bi, p, i: (bi, i, col0 + p)),
            pl.BlockSpec((1, s, PAIR), lambda bi, p, i: (bi, 0, col0 + N_PAIRS + p)),
            pl.BlockSpec((1, 2, VT_ROWS, s), lambda bi, p, i: (bi, p, 0, 0)),
            pl.BlockSpec((1, 2, (2 * nblk - 1) * tq, tq), lambda bi, p, i: (layer, p, 0, 0)),
        ],
        out_specs=pl.BlockSpec((1, nsub * tq, PAIR), lambda bi, p, i: (bi, i, p)),
        out_shape=jax.ShapeDtypeStruct((b, s, WIDTH), BF16),
        compiler_params=_params("parallel", "parallel", "arbitrary"),
        name="chunk_attn",
    )(proj3, proj3, vtc, bias_tables)


def _mix_kernel(oa_ref, oc_ref, ga_ref, gc_ref, x_ref, mod_ref, wa_ref, wc_ref, wo_ref,
                g_ref, b_ref, o_ref, *, alpha):
    ya = _dot(oa_ref[...], wa_ref[...])
    yc = _dot(oc_ref[...], wc_ref[...])
    merged = (jax.nn.sigmoid(ga_ref[...].astype(F32)) * ya
              + jax.nn.sigmoid(gc_ref[...].astype(F32)) * yc)
    mix = _dot(merged.astype(BF16), wo_ref[...])
    z = alpha * x_ref[...] + (1.0 + mod_ref[0, 2:3, :]) * mix
    o_ref[...] = _ln(z) * g_ref[...] + b_ref[...]


def _mix(oa, oc, proj, xr, mod, w_a, w_c, w_o, gain, bias, seq, alpha):
    rows, d = xr.shape
    tm = TM_MIX
    tiles_per_seq = seq // tm
    gate_blk = (2 * 3 * WIDTH) // d
    const = lambda i: (0, 0)
    return pl.pallas_call(
        functools.partial(_mix_kernel, alpha=alpha),
        grid=(rows // tm,),
        in_specs=[
            pl.BlockSpec((tm, WIDTH), lambda i: (i, 0)),
            pl.BlockSpec((tm, WIDTH), lambda i: (i, 0)),
            pl.BlockSpec((tm, d), lambda i: (i, gate_blk)),
            pl.BlockSpec((tm, d), lambda i: (i, gate_blk + 1)),
            pl.BlockSpec((tm, d), lambda i: (i, 0)),
            pl.BlockSpec((1, N_MOD, d), lambda i: (i // tiles_per_seq, 0, 0)),
            pl.BlockSpec((WIDTH, d), const),
            pl.BlockSpec((WIDTH, d), const),
            pl.BlockSpec((d, d), const),
            pl.BlockSpec((1, d), const),
            pl.BlockSpec((1, d), const),
        ],
        out_specs=pl.BlockSpec((tm, d), lambda i: (i, 0)),
        out_shape=jax.ShapeDtypeStruct((rows, d), F32),
        compiler_params=_params("parallel"),
        name="mix_out",
    )(oa, oc, proj, proj, xr, mod, w_a, w_c, w_o, gain, bias)


def _ffn_kernel(x_ref, xh_ref, mod_ref, wu_ref, cw_ref, cb_ref, wd_ref, g_ref, b_ref, o_ref, h_ref, acc_ref,
                *, alpha, tm, ck):
    t = pl.program_id(1)
    d_ff = wd_ref.shape[0]
    x = x_ref[0]
    sc = 1.0 + mod_ref[0, 4:5, :]
    sh = mod_ref[0, 3:4, :]
    h_ref[HALO:, :] = (_ln(x) * sc + sh).astype(BF16)
    keep = jnp.where(t > 0, 1.0, 0.0)
    h_ref[0:HALO, :] = ((_ln(xh_ref[0]) * sc + sh) * keep).astype(BF16)
    acc_ref[...] = jnp.zeros_like(acc_ref)

    def conv(u, lo_col):
        cols = slice(lo_col, lo_col + ck)
        lo = HALO - (CONV_WIDTH - 1)
        y = cb_ref[:, cols]
        for j in range(CONV_WIDTH):
            y = y + cw_ref[j:j + 1, cols] * u[lo + j:lo + j + tm, :]
        return y

    def up(c):
        he = h_ref[...]
        return (_dot(he, wu_ref[:, c * ck:(c + 1) * ck]),
                _dot(he, wu_ref[:, d_ff + c * ck:d_ff + (c + 1) * ck]))

    u = up(0)
    for c in range(d_ff // ck):
        u_next = up(c + 1) if (c + 1) * ck < d_ff else None
        a = conv(u[0], c * ck)
        v = conv(u[1], d_ff + c * ck)
        act = (a * jax.nn.sigmoid(a) * v).astype(BF16)
        acc_ref[...] += _dot(act, wd_ref[c * ck:(c + 1) * ck, :])
        u = u_next
    z = alpha * x + (1.0 + mod_ref[0, 5:6, :]) * acc_ref[...]
    o_ref[0] = _ln(z) * g_ref[...] + b_ref[...]


def _ffn(x3, mod, w_up, conv_w, conv_b, w_down, gain, bias, alpha):
    b, s, d = x3.shape
    tm = TM_FFN
    d_ff = w_down.shape[0]
    assert d_ff % CK_FFN == 0
    halo_per_tile = tm // HALO
    c2 = lambda bi, t: (0, 0)
    return pl.pallas_call(
        functools.partial(_ffn_kernel, alpha=alpha, tm=tm, ck=CK_FFN),
        grid=(b, s // tm),
        in_specs=[
            pl.BlockSpec((1, tm, d), lambda bi, t: (bi, t, 0)),
            pl.BlockSpec((1, HALO, d), lambda bi, t: (bi, jnp.maximum(t * halo_per_tile - 1, 0), 0)),
            pl.BlockSpec((1, N_MOD, d), lambda bi, t: (bi, 0, 0)),
            pl.BlockSpec((d, 2 * d_ff), c2),
            pl.BlockSpec((CONV_WIDTH, 2 * d_ff), c2),
            pl.BlockSpec((1, 2 * d_ff), c2),
            pl.BlockSpec((d_ff, d), c2),
            pl.BlockSpec((1, d), c2),
            pl.BlockSpec((1, d), c2),
        ],
        out_specs=pl.BlockSpec((1, tm, d), lambda bi, t: (bi, t, 0)),
        out_shape=jax.ShapeDtypeStruct((b, s, d), F32),
        scratch_shapes=[pltpu.VMEM((tm + HALO, d), BF16), pltpu.VMEM((tm, d), F32)],
        compiler_params=_params("parallel", "parallel"),
        name="conv_ffn",
    )(x3, x3, mod, w_up, conv_w, conv_b, w_down, gain, bias)


def kernel(x, c, w_in, b_f, rel_bias, w_br_fox, w_br_chunk, w_out, w_up, conv_w, conv_b, w_down,
           w_ada, b_ada, ln1_g, ln1_b, ln2_g, ln2_b):
    b, s, d = x.shape
    depth = w_in.shape[0]
    alpha = (2.0 * depth) ** 0.25
    rows = b * s
    mod_all = _ada_mod(c, w_ada, b_ada).reshape(depth, b, N_MOD, d)
    bias_tables = _chunk_bias_tables(rel_bias, TQ_CHUNK)

    f0 = 3 * WIDTH
    f1 = f0 + N_HEADS
    xr = x.reshape(rows, d)
    for l in range(depth):
        mod = mod_all[l]
        q_scale = LOG2E / math.sqrt(HEAD_DIM)
        qc0 = f1
        qc1 = f1 + WIDTH
        w_main = jnp.concatenate([w_in[l, :, :WIDTH] * q_scale, w_in[l, :, WIDTH:f0],
                                  w_in[l, :, qc0:qc1] * q_scale, w_in[l, :, qc1:]], axis=1).astype(BF16)
        w_f = jnp.pad(w_in[l, :, f0:f1], ((0, 0), (0, LANES - N_HEADS))).astype(BF16)
        bf_pad = jnp.pad(b_f[l], (0, LANES - N_HEADS)).reshape(1, LANES)

        proj, f = _ln_proj(xr, mod, w_main, w_f, s)
        proj3 = proj.reshape(b, s, MAIN_COLS)
        cqt, fs, vt, vtc = _attn_prep(f.reshape(b, s, LANES), bf_pad, proj3)
        o_a = _fox(proj3, fs, vt, cqt)
        o_c = _chunk_attn(proj3, vtc, bias_tables, l)
        xr = _mix(o_a.reshape(rows, WIDTH), o_c.reshape(rows, WIDTH), proj, xr, mod,
                  w_br_fox[l].astype(BF16), w_br_chunk[l].astype(BF16), w_out[l].astype(BF16),
                  ln1_g[l].reshape(1, d), ln1_b[l].reshape(1, d), s, alpha)

        x3 = _ffn(xr.reshape(b, s, d), mod, w_up[l].astype(BF16), conv_w[l], conv_b[l].reshape(1, -1),
                  w_down[l].astype(BF16), ln2_g[l].reshape(1, d), ln2_b[l].reshape(1, d), alpha)
        xr = x3.reshape(rows, d)
    return xr.reshape(b, s, d)
```
